```python
import math
import jax
import jax.numpy as jnp
from jax import lax

D_MODEL = 2048
BATCH = 1
SEQ = 8192
DEPTH = 2
DEC_BATCH = 128
DEC_SEQ = 8
PAST_LEN = 2048
PAGE_SIZE = 128

MIX_WIDTH = D_MODEL
SSM_WIDTH = MIX_WIDTH // 4
SSM_GROUP_CH = 16
SSM_GROUPS = SSM_WIDTH // SSM_GROUP_CH
SSM_STATE = 64
POOL_WIDTH = MIX_WIDTH // 4
POOL_WINDOWS = (2, 4, 8, 16)
N_POOL = len(POOL_WINDOWS)
POOL_CH = POOL_WIDTH // N_POOL
POOL_BUF = max(POOL_WINDOWS) - 1
NSA_WIDTH = MIX_WIDTH - SSM_WIDTH - POOL_WIDTH
HEAD_DIM = 64
N_HEADS = NSA_WIDTH // HEAD_DIM
N_KV = 4
GQA = N_HEADS // N_KV
KV_WIDTH = N_KV * HEAD_DIM
CMP_STRIDE = 16
CMP_BLOCK = 32
CMP_RATIO = CMP_BLOCK // CMP_STRIDE
CMP_HIDDEN = 2 * HEAD_DIM
SEL_BLOCK = 64
TOP_N = 16
WINDOW = 512
Q_BLOCK = 128
D_FF = 4 * D_MODEL
ALPHA = (2 * DEPTH) ** 0.25
BETA = (8 * DEPTH) ** -0.25
LN_EPS = 1e-5
NEG_INF = -1e30
TINY = 1e-30
FORCED_SCORE = 1e4
PROJ_SIZES = (SSM_WIDTH, POOL_WIDTH, NSA_WIDTH) + (KV_WIDTH,) * 6 + (3 * N_HEADS,)
PROJ_WIDTH = sum(PROJ_SIZES)
SPLIT_POINTS = tuple(sum(PROJ_SIZES[:i + 1]) for i in range(len(PROJ_SIZES) - 1))

kernel_name = 'hybrid_s5_pool_nsa_step'


def alibi_slopes():
    return jnp.exp2(-8.0 * (jnp.arange(N_HEADS, dtype=jnp.float32) + 1.0) / N_HEADS)


def layer_norm(x, g, b):
    xf = x.astype(jnp.float32)
    xc = xf - jnp.mean(xf, axis=-1, keepdims=True)
    var = jnp.mean(xc * xc, axis=-1, keepdims=True)
    return (xc * lax.rsqrt(var + LN_EPS) * g.astype(jnp.float32) + b.astype(jnp.float32)).astype(x.dtype)


def masked_softmax(s, mask):
    s = jnp.where(mask, s, NEG_INF)
    p = jnp.exp(s - jnp.max(s, axis=-1, keepdims=True)) * mask
    return p / jnp.maximum(jnp.sum(p, axis=-1, keepdims=True), TINY)


def _complex_affine_combine(e1, e2):
    a1r, a1i, b1r, b1i = e1
    a2r, a2i, b2r, b2i = e2
    return (a2r * a1r - a2i * a1i, a2r * a1i + a2i * a1r,
            a2r * b1r - a2i * b1i + b2r, a2r * b1i + a2i * b1r + b2i)


def s5_mixer(u, h0_re, h0_im, a_re, a_im, log_dt, b_re, b_im, c_re, c_im, d, w_glu):
    f32 = jnp.float32
    bsz, L, _ = u.shape
    uf = u.astype(f32).reshape(bsz, L, SSM_GROUPS, SSM_GROUP_CH)
    ar, ai = a_re.astype(f32), a_im.astype(f32)
    dt = jnp.exp(log_dt.astype(f32))[:, None]
    mag = jnp.exp(ar * dt)
    abar_re, abar_im = mag * jnp.cos(ai * dt), mag * jnp.sin(ai * dt)
    den = ar * ar + ai * ai
    zr = ((abar_re - 1.0) * ar + abar_im * ai) / den
    zi = (abar_im * ar - (abar_re - 1.0) * ai) / den
    bu_re = jnp.einsum('gph,blgh->blgp', b_re.astype(f32), uf)
    bu_im = jnp.einsum('gph,blgh->blgp', b_im.astype(f32), uf)
    x_re = zr * bu_re - zi * bu_im
    x_im = zr * bu_im + zi * bu_re
    h0r, h0i = h0_re.astype(f32), h0_im.astype(f32)
    x_re = x_re.at[:, 0].add(abar_re * h0r - abar_im * h0i)
    x_im = x_im.at[:, 0].add(abar_re * h0i + abar_im * h0r)
    a_seq_re = jnp.broadcast_to(abar_re, x_re.shape)
    a_seq_im = jnp.broadcast_to(abar_im, x_im.shape)
    _, _, h_re, h_im = lax.associative_scan(_complex_affine_combine, (a_seq_re, a_seq_im, x_re, x_im), axis=1)
    y = jnp.einsum('ghp,blgp->blgh', c_re.astype(f32), h_re) - jnp.einsum('ghp,blgp->blgh', c_im.astype(f32), h_im)
    y = y + d.astype(f32).reshape(SSM_GROUPS, SSM_GROUP_CH) * uf
    z = jax.nn.gelu(y.reshape(bsz, L, SSM_WIDTH))
    out = z * jax.nn.sigmoid(z @ w_glu.astype(f32))
    return out, h_re[:, -1], h_im[:, -1]


def pool_mixer(u, buf, pos0, w, scale):
    f32 = jnp.float32
    bsz, L, _ = u.shape
    xcat = jnp.concatenate([buf.astype(u.dtype), u], axis=1)
    cs = jnp.cumsum(xcat.astype(f32), axis=1)
    cs = jnp.pad(cs, ((0, 0), (1, 0), (0, 0))).reshape(bsz, POOL_BUF + L + 1, N_POOL, POOL_CH)
    pos = pos0 + jnp.arange(L)
    hi = POOL_BUF + 1
    pooled = jnp.stack([
        (cs[:, hi:hi + L, gi] - cs[:, hi - wd:hi - wd + L, gi])
        / jnp.minimum(wd, pos + 1).astype(f32)[None, :, None]
        for gi, wd in enumerate(POOL_WINDOWS)], axis=2)
    mix = pooled - u.astype(f32).reshape(bsz, L, N_POOL, POOL_CH)
    y = jnp.einsum('blgc,gce->blge', mix, w.astype(f32)).reshape(bsz, L, POOL_WIDTH) * scale.astype(f32)
    return y, xcat[:, -POOL_BUF:]


def compress_blocks(k, pe, w1, w2):
    f32 = jnp.float32
    bsz, T = k.shape[:2]
    n_chunk = T // CMP_STRIDE
    n_cmp = n_chunk - CMP_RATIO + 1
    chunks = k[:, :n_chunk * CMP_STRIDE].reshape(bsz, n_chunk, CMP_STRIDE, N_KV, HEAD_DIM)
    w1b = w1.reshape(CMP_BLOCK, HEAD_DIM, CMP_HIDDEN)
    hid = jnp.einsum('sd,sdh->h', pe, w1b, preferred_element_type=f32)
    for m in range(CMP_RATIO):
        part = jnp.einsum('bcskd,sdh->bckh', chunks, w1b[m * CMP_STRIDE:(m + 1) * CMP_STRIDE],
                          preferred_element_type=f32)
        hid = hid + part[:, m:m + n_cmp]
    return (jax.nn.gelu(hid) @ w2.astype(f32)).astype(k.dtype)


def nsa_block(q, gate, qpos, kc, vc, c_end, c_mid, overlap, ks_b, vs_b, kw, vw, wpos, slopes):
    f32 = jnp.float32
    bsz, lq = q.shape[:2]
    qg = q.reshape(bsz, lq, N_KV, GQA, HEAD_DIM) * (HEAD_DIM ** -0.5)
    t = qpos.astype(f32)
    m = slopes.reshape(N_KV, GQA)[None, :, :, None, None]
    s_c = jnp.einsum('bqkgd,bnkd->bkgqn', qg, kc, preferred_element_type=f32)
    s_c = s_c - m * (t[:, None] - c_mid[None, :])
    p_c = masked_softmax(s_c, c_end[None, :] <= qpos[:, None])
    o_c = jnp.einsum('bkgqn,bnkd->bkgqd', p_c.astype(vc.dtype), vc, preferred_element_type=f32)
    imp = jnp.einsum('bkgqn,ns->bkqs', p_c, overlap)
    n_sel = ks_b.shape[2]
    sj = jnp.arange(n_sel)[None, :]
    cur = (qpos // SEL_BLOCK)[:, None]
    visible = sj <= cur
    forced = visible & ((sj == 0) | (sj == cur) | (sj == cur - 1))
    score = jnp.where(forced, FORCED_SCORE, jnp.where(visible, imp, -1.0))
    top_val, top_idx = lax.top_k(score, min(TOP_N, n_sel))
    valid = top_val >= 0.0
    bi = jnp.arange(bsz)[:, None, None, None]
    ki = jnp.arange(N_KV)[None, :, None, None]
    k_g = ks_b[bi, ki, top_idx]
    v_g = vs_b[bi, ki, top_idx]
    kpos = top_idx[..., None] * SEL_BLOCK + jnp.arange(SEL_BLOCK)
    mask_s = (kpos <= qpos[:, None, None]) & valid[..., None]
    s_s = jnp.einsum('bqkgd,bkqnsd->bkgqns', qg, k_g, preferred_element_type=f32)
    s_s = s_s - m[..., None] * (t[:, None, None] - kpos[:, :, None].astype(f32))
    n_keys = s_s.shape[-2] * SEL_BLOCK
    p_s = masked_softmax(s_s.reshape(bsz, N_KV, GQA, lq, n_keys),
                         mask_s[:, :, None].reshape(bsz, N_KV, 1, lq, n_keys))
    o_s = jnp.einsum('bkgqm,bkqmd->bkgqd', p_s.astype(v_g.dtype),
                     v_g.reshape(bsz, N_KV, lq, n_keys, HEAD_DIM), preferred_element_type=f32)
    s_w = jnp.einsum('bqkgd,bwkd->bkgqw', qg, kw, preferred_element_type=f32)
    s_w = s_w - m * (t[:, None] - wpos[None, :].astype(f32))
    dist = qpos[:, None] - wpos[None, :]
    p_w = masked_softmax(s_w, (dist >= 0) & (dist < WINDOW) & (wpos[None, :] >= 0))
    o_w = jnp.einsum('bkgqw,bwkd->bkgqd', p_w.astype(vw.dtype), vw, preferred_element_type=f32)
    o = jnp.stack([o_c, o_s, o_w], axis=-1).transpose(0, 3, 1, 2, 4, 5)
    o = o.reshape(bsz, lq, N_HEADS, HEAD_DIM, 3)
    return jnp.einsum('bqhdc,bqhc->bqhd', o, gate)


def nsa_mixer(q, gate, kc_all, vc_all, ks_all, vs_all, kw_all, vw_all, pos0, qblk, cmp_pe, cmp_w1, cmp_w2):
    f32 = jnp.float32
    bsz, L = q.shape[:2]
    kc = compress_blocks(kc_all, cmp_pe[0], cmp_w1[0], cmp_w2[0])
    vc = compress_blocks(vc_all, cmp_pe[1], cmp_w1[1], cmp_w2[1])
    ci = jnp.arange(kc.shape[1])
    c_end = ci * CMP_STRIDE + (CMP_BLOCK - 1)
    c_mid = (ci * CMP_STRIDE).astype(f32) + (CMP_BLOCK - 1) / 2.0
    T = ks_all.shape[1]
    n_sel = -(-T // SEL_BLOCK)
    sj = jnp.arange(n_sel)
    overlap = ((ci[:, None] * CMP_STRIDE < (sj[None, :] + 1) * SEL_BLOCK)
               & (c_end[:, None] >= sj[None, :] * SEL_BLOCK)).astype(f32)

    def to_blocks(k):
        k = jnp.pad(k, ((0, 0), (0, n_sel * SEL_BLOCK - T), (0, 0), (0, 0)))
        return k.reshape(bsz, n_sel, SEL_BLOCK, N_KV, HEAD_DIM).transpose(0, 3, 1, 2, 4)

    ks_b, vs_b = to_blocks(ks_all), to_blocks(vs_all)
    slopes = alibi_slopes()

    def one_block(i):
        s0 = i * qblk
        q_b = lax.dynamic_slice_in_dim(q, s0, qblk, axis=1)
        g_b = lax.dynamic_slice_in_dim(gate, s0, qblk, axis=1)
        kw_b = lax.dynamic_slice_in_dim(kw_all, s0, qblk + WINDOW, axis=1)
        vw_b = lax.dynamic_slice_in_dim(vw_all, s0, qblk + WINDOW, axis=1)
        qpos = pos0 + s0 + jnp.arange(qblk)
        wpos = qpos[0] - WINDOW + jnp.arange(qblk + WINDOW)
        return nsa_block(q_b, g_b, qpos, kc, vc, c_end, c_mid, overlap, ks_b, vs_b, kw_b, vw_b, wpos, slopes)

    out = lax.map(one_block, jnp.arange(L // qblk))
    return out.transpose(1, 0, 2, 3, 4).reshape(bsz, L, NSA_WIDTH)


def gather_pages(pool, page_table):
    rows = pool[page_table]
    return rows.reshape(rows.shape[0], -1, N_KV, HEAD_DIM)


def empty_past(batch, dtype):
    kv0 = jnp.zeros((batch, 0, N_KV, HEAD_DIM), dtype)
    h0 = jnp.zeros((batch, SSM_GROUPS, SSM_STATE), jnp.float32)
    return (h0, h0, jnp.zeros((batch, POOL_BUF, POOL_WIDTH), dtype), kv0, kv0, kv0, kv0, kv0, kv0)


def hybrid_layer(x, pos0, qblk, past, w):
    (w_in, ssm_a_re, ssm_a_im, ssm_log_dt, ssm_b_re, ssm_b_im, ssm_c_re, ssm_c_im, ssm_d, ssm_w_glu,
     pool_w, pool_scale, cmp_pe, cmp_w1, cmp_w2, w_out, ln1_g, ln1_b, mlp_w1, mlp_w2, ln2_g, ln2_b) = w
    (h0_re, h0_im, pool_buf, kc_p, vc_p, ks_p, vs_p, kw_p, vw_p) = past
    bsz, L, _ = x.shape
    parts = jnp.split(x @ w_in, SPLIT_POINTS, axis=-1)
    u_ssm, u_pool = parts[0], parts[1]
    q = parts[2].reshape(bsz, L, N_HEADS, HEAD_DIM)
    kc, vc, ks, vs, kw, vw = [p.reshape(bsz, L, N_KV, HEAD_DIM) for p in parts[3:9]]
    gate = jax.nn.sigmoid(parts[9].astype(jnp.float32)).reshape(bsz, L, N_HEADS, 3)

    y_ssm, h_re, h_im = s5_mixer(u_ssm, h0_re, h0_im, ssm_a_re, ssm_a_im, ssm_log_dt, ssm_b_re, ssm_b_im,
                                 ssm_c_re, ssm_c_im, ssm_d, ssm_w_glu)
    y_pool, pool_new = pool_mixer(u_pool, pool_buf, pos0, pool_w, pool_scale)
    kw_cat = jnp.concatenate([kw_p, kw], axis=1)
    vw_cat = jnp.concatenate([vw_p, vw], axis=1)
    front = ((0, 0), (WINDOW - kw_p.shape[1], 0), (0, 0), (0, 0))
    y_nsa = nsa_mixer(q, gate,
                      jnp.concatenate([kc_p, kc], axis=1), jnp.concatenate([vc_p, vc], axis=1),
                      jnp.concatenate([ks_p, ks], axis=1), jnp.concatenate([vs_p, vs], axis=1),
                      jnp.pad(kw_cat, front), jnp.pad(vw_cat, front), pos0, qblk, cmp_pe, cmp_w1, cmp_w2)
    mixed = jnp.concatenate([y_ssm.astype(x.dtype), y_pool.astype(x.dtype), y_nsa.astype(x.dtype)], axis=-1) @ w_out
    x = layer_norm(ALPHA * x + mixed, ln1_g, ln1_b)
    ff = jnp.square(jax.nn.relu(x @ mlp_w1)) @ mlp_w2
    x = layer_norm(ALPHA * x + ff, ln2_g, ln2_b)
    n_keep = min(WINDOW, kw_cat.shape[1])
    new = (kc, vc, ks, vs, kw_cat[:, -n_keep:], vw_cat[:, -n_keep:], h_re, h_im, pool_new)
    return x, new


def setup_inputs(seed: int = 0) -> dict:
    key = jax.random.key(seed)
    keys = iter(jax.random.split(key, 48))
    f32 = jnp.float32

    def nrm(shape, scale=1.0):
        return jax.random.normal(next(keys), shape, f32) * scale

    n_pages = PAST_LEN // PAGE_SIZE
    n_used = DEC_BATCH * n_pages
    n_phys = n_used + max(1, n_used // 4)
    win_cache = min(WINDOW, PAST_LEN)
    page_shape = (DEPTH, n_phys, PAGE_SIZE, N_KV, HEAD_DIM)
    win_shape = (DEPTH, DEC_BATCH, win_cache, N_KV, HEAD_DIM)
    ssm_shape = (DEPTH, DEC_BATCH, SSM_GROUPS, SSM_STATE)
    inp = {}
    inp['x_prompt'] = nrm((BATCH, SEQ, D_MODEL))
    inp['x_sample'] = nrm((DEC_BATCH, DEC_SEQ, D_MODEL))
    inp['cache_k_cmp'] = nrm(page_shape)
    inp['cache_v_cmp'] = nrm(page_shape)
    inp['cache_k_sel'] = nrm(page_shape)
    inp['cache_v_sel'] = nrm(page_shape)
    inp['cache_k_win'] = nrm(win_shape)
    inp['cache_v_win'] = nrm(win_shape)
    inp['state_ssm_re'] = nrm(ssm_shape, 0.1)
    inp['state_ssm_im'] = nrm(ssm_shape, 0.1)
    inp['state_pool'] = nrm((DEPTH, DEC_BATCH, POOL_BUF, POOL_WIDTH))
    inp['page_table'] = jax.random.permutation(next(keys), n_phys)[:n_used].reshape(DEC_BATCH, n_pages).astype(jnp.int32)
    inp['w_in'] = nrm((DEPTH, D_MODEL, PROJ_WIDTH), D_MODEL ** -0.5)
    inp['ssm_a_re'] = -0.5 + nrm((DEPTH, SSM_GROUPS, SSM_STATE), 0.01)
    inp['ssm_a_im'] = math.pi * jnp.arange(SSM_STATE, dtype=f32) + nrm((DEPTH, SSM_GROUPS, SSM_STATE), 0.01)
    inp['ssm_log_dt'] = jax.random.uniform(next(keys), (DEPTH, SSM_GROUPS), f32, math.log(1e-3), math.log(1e-1))
    inp['ssm_b_re'] = nrm((DEPTH, SSM_GROUPS, SSM_STATE, SSM_GROUP_CH), (2 * SSM_GROUP_CH) ** -0.5)
    inp['ssm_b_im'] = nrm((DEPTH, SSM_GROUPS, SSM_STATE, SSM_GROUP_CH), (2 * SSM_GROUP_CH) ** -0.5)
    inp['ssm_c_re'] = nrm((DEPTH, SSM_GROUPS, SSM_GROUP_CH, SSM_STATE), 0.5 ** 0.5)
    inp['ssm_c_im'] = nrm((DEPTH, SSM_GROUPS, SSM_GROUP_CH, SSM_STATE), 0.5 ** 0.5)
    inp['ssm_d'] = nrm((DEPTH, SSM_WIDTH))
    inp['ssm_w_glu'] = nrm((DEPTH, SSM_WIDTH, SSM_WIDTH), SSM_WIDTH ** -0.5)
    inp['pool_w'] = nrm((DEPTH, N_POOL, POOL_CH, POOL_CH), POOL_CH ** -0.5)
    inp['pool_scale'] = 1.0 + nrm((DEPTH, POOL_WIDTH), 0.1)
    inp['cmp_pe'] = nrm((DEPTH, 2, CMP_BLOCK, HEAD_DIM), 0.1)
    inp['cmp_w1'] = nrm((DEPTH, 2, CMP_BLOCK * HEAD_DIM, CMP_HIDDEN), (CMP_BLOCK * HEAD_DIM) ** -0.5)
    inp['cmp_w2'] = nrm((DEPTH, 2, CMP_HIDDEN, HEAD_DIM), CMP_HIDDEN ** -0.5)
    inp['w_out'] = nrm((DEPTH, MIX_WIDTH, D_MODEL), BETA * MIX_WIDTH ** -0.5)
    inp['ln1_g'] = 1.0 + nrm((DEPTH, D_MODEL), 0.05)
    inp['ln1_b'] = nrm((DEPTH, D_MODEL), 0.02)
    inp['mlp_w1'] = nrm((DEPTH, D_MODEL, D_FF), D_MODEL ** -0.5)
    inp['mlp_w2'] = nrm((DEPTH, D_FF, D_MODEL), BETA * D_FF ** -0.5)
    inp['ln2_g'] = 1.0 + nrm((DEPTH, D_MODEL), 0.05)
    inp['ln2_b'] = nrm((DEPTH, D_MODEL), 0.02)
    return inp


def reference(x_prompt, x_sample, cache_k_cmp, cache_v_cmp, cache_k_sel, cache_v_sel, cache_k_win, cache_v_win,
              state_ssm_re, state_ssm_im, state_pool, page_table, w_in, ssm_a_re, ssm_a_im, ssm_log_dt,
              ssm_b_re, ssm_b_im, ssm_c_re, ssm_c_im, ssm_d, ssm_w_glu, pool_w, pool_scale, cmp_pe, cmp_w1,
              cmp_w2, w_out, ln1_g, ln1_b, mlp_w1, mlp_w2, ln2_g, ln2_b):
    weights = (w_in, ssm_a_re, ssm_a_im, ssm_log_dt, ssm_b_re, ssm_b_im, ssm_c_re, ssm_c_im, ssm_d, ssm_w_glu,
               pool_w, pool_scale, cmp_pe, cmp_w1, cmp_w2, w_out, ln1_g, ln1_b, mlp_w1, mlp_w2, ln2_g, ln2_b)
    y_p, y_s = x_prompt, x_sample
    new_p, new_s = [], []
    for l in range(DEPTH):
        lw = tuple(wt[l] for wt in weights)
        y_p, st_p = hybrid_layer(y_p, 0, min(Q_BLOCK, y_p.shape[1]), empty_past(y_p.shape[0], y_p.dtype), lw)
        past_s = (state_ssm_re[l], state_ssm_im[l], state_pool[l],
                  gather_pages(cache_k_cmp[l], page_table), gather_pages(cache_v_cmp[l], page_table),
                  gather_pages(cache_k_sel[l], page_table), gather_pages(cache_v_sel[l], page_table),
                  cache_k_win[l], cache_v_win[l])
        y_s, st_s = hybrid_layer(y_s, PAST_LEN, 1, past_s, lw)
        new_p.append(st_p)
        new_s.append(st_s)
    (p_kc, p_vc, p_ks, p_vs, p_kw, p_vw, p_hr, p_hi, p_pool) = [jnp.stack(f) for f in zip(*new_p)]
    (s_kc, s_vc, s_ks, s_vs, s_kw, s_vw, s_hr, s_hi, s_pool) = [jnp.stack(f) for f in zip(*new_s)]
    return (y_p, y_s, p_kc, s_kc, p_vc, s_vc, p_ks, s_ks, p_vs, s_vs, p_kw, s_kw, p_vw, s_vw,
            p_hr, s_hr, p_hi, s_hi, p_pool, s_pool)
```

```python
import functools
import math

import jax
import jax.numpy as jnp
from jax import lax
from jax.experimental import pallas as pl
from jax.experimental.pallas import tpu as pltpu

F32 = jnp.float32
BF16 = jnp.bfloat16

D_MODEL = 2048
DEPTH = 2
PAGE_SIZE = 128
SSM_WIDTH = 512
SSM_GROUP_CH = 16
SSM_GROUPS = 32
SSM_STATE = 64
SSM_FLAT = SSM_GROUPS * SSM_STATE
POOL_WIDTH = 512
POOL_WINDOWS = (2, 4, 8, 16)
POOL_CH = 128
POOL_BUF = 15
NSA_WIDTH = 1024
HEAD_DIM = 64
N_HEADS = 16
N_KV = 4
GQA = 4
KV_WIDTH = N_KV * HEAD_DIM
CMP_STRIDE = 16
CMP_BLOCK = 32
CMP_HIDDEN = 128
SEL_BLOCK = 64
TOP_N = 16
WINDOW = 512
Q_BLOCK = 128
D_FF = 4 * D_MODEL
ALPHA = (2 * DEPTH) ** 0.25
LN_EPS = 1e-5
NEG_INF = -1e30
TINY = 1e-30
FORCED_SCORE = 1e4
PROJ_SIZES = (SSM_WIDTH, POOL_WIDTH, NSA_WIDTH) + (KV_WIDTH,) * 6 + (3 * N_HEADS,)
PROJ_WIDTH = sum(PROJ_SIZES)
PROJ_PAD = 3840
GATE_COL = 3584

LANES = 128
SUBLANES = 8
VMEM_LIMIT = 56 * 1024 * 1024


def _cparams(*sem):
    return pltpu.CompilerParams(dimension_semantics=sem, vmem_limit_bytes=VMEM_LIMIT)


def _gelu(x):
    return 0.5 * x * (1.0 + jnp.tanh(math.sqrt(2.0 / math.pi) * (x + 0.044715 * (x * x * x))))


def _sigmoid(x):
    return 1.0 / (1.0 + jnp.exp(-x))


def _layer_norm(z, g, b):
    zc = z - jnp.mean(z, axis=-1, keepdims=True)
    var = jnp.mean(zc * zc, axis=-1, keepdims=True)
    return zc * lax.rsqrt(var + LN_EPS) * g + b


def _proj_kernel(x_ref, w_ref, o_ref, xb_ref):
    @pl.when(pl.program_id(1) == 0)
    def _():
        xb_ref[...] = x_ref[...].astype(BF16)

    o_ref[...] = jnp.dot(xb_ref[...], w_ref[...], preferred_element_type=F32)


def _input_projection(x, w, tm=1024, tn=768):
    m, k = x.shape
    n = w.shape[1]
    return pl.pallas_call(
        _proj_kernel,
        grid=(m // tm, n // tn),
        in_specs=[pl.BlockSpec((tm, k), lambda i, j: (i, 0)),
                  pl.BlockSpec((k, tn), lambda i, j: (0, j))],
        out_specs=pl.BlockSpec((tm, tn), lambda i, j: (i, j)),
        out_shape=jax.ShapeDtypeStruct((m, n), F32),
        scratch_shapes=[pltpu.VMEM((tm, k), BF16)],
        compiler_params=_cparams("parallel", "arbitrary"),
        name="input_projection",
    )(x, w)


def _outproj_ln_kernel(x_ref, ys_ref, yp_ref, yn_ref, w_ref, g_ref, b_ref, o_ref):
    acc = jnp.dot(ys_ref[...], w_ref[0:SSM_WIDTH, :], preferred_element_type=F32)
    acc += jnp.dot(yp_ref[...], w_ref[SSM_WIDTH:SSM_WIDTH + POOL_WIDTH, :], preferred_element_type=F32)
    acc += jnp.dot(yn_ref[...], w_ref[SSM_WIDTH + POOL_WIDTH:, :], preferred_element_type=F32)
    o_ref[...] = _layer_norm(ALPHA * x_ref[...] + acc, g_ref[...], b_ref[...])


def _output_projection_ln(x, y_ssm, y_pool, y_nsa, w, g, b, tm=512):
    m, d = x.shape
    row = lambda i: (i, 0)
    fixed = lambda i: (0, 0)
    return pl.pallas_call(
        _outproj_ln_kernel,
        grid=(m // tm,),
        in_specs=[pl.BlockSpec((tm, d), row),
                  pl.BlockSpec((tm, SSM_WIDTH), row),
                  pl.BlockSpec((tm, POOL_WIDTH), row),
                  pl.BlockSpec((tm, NSA_WIDTH), row),
                  pl.BlockSpec((d, d), fixed),
                  pl.BlockSpec((1, d), fixed),
                  pl.BlockSpec((1, d), fixed)],
        out_specs=pl.BlockSpec((tm, d), row),
        out_shape=jax.ShapeDtypeStruct((m, d), F32),
        compiler_params=_cparams("parallel"),
        name="output_projection_ln",
    )(x, y_ssm, y_pool, y_nsa, w, g, b)


def _mlp_ln_kernel(x_ref, w1_ref, w2_ref, g_ref, b_ref, o_ref, xb_ref, acc_ref):
    f = pl.program_id(1)

    @pl.when(f == 0)
    def _():
        xb_ref[...] = x_ref[...].astype(BF16)
        acc_ref[...] = jnp.zeros_like(acc_ref)

    h = jnp.dot(xb_ref[...], w1_ref[...], preferred_element_type=F32)
    h = jnp.square(jnp.maximum(h, 0.0)).astype(BF16)
    acc_ref[...] += jnp.dot(h, w2_ref[...], preferred_element_type=F32)

    @pl.when(f == pl.num_programs(1) - 1)
    def _():
        o_ref[...] = _layer_norm(ALPHA * x_ref[...] + acc_ref[...], g_ref[...], b_ref[...])


def _mlp_ln(x, w1, w2, g, b, tm=512, tf=512):
    m, d = x.shape
    ff = w1.shape[1]
    return pl.pallas_call(
        _mlp_ln_kernel,
        grid=(m // tm, ff // tf),
        in_specs=[pl.BlockSpec((tm, d), lambda i, f: (i, 0)),
                  pl.BlockSpec((d, tf), lambda i, f: (0, f)),
                  pl.BlockSpec((tf, d), lambda i, f: (f, 0)),
                  pl.BlockSpec((1, d), lambda i, f: (0, 0)),
                  pl.BlockSpec((1, d), lambda i, f: (0, 0))],
        out_specs=pl.BlockSpec((tm, d), lambda i, f: (i, 0)),
        out_shape=jax.ShapeDtypeStruct((m, d), F32),
        scratch_shapes=[pltpu.VMEM((tm, d), BF16), pltpu.VMEM((tm, d), F32)],
        compiler_params=_cparams("parallel", "arbitrary"),
        name="mlp_ln",
    )(x, w1, w2, g, b)


S5_GROUP_BLOCKS = 4
S5_BLOCK_LANES = SSM_FLAT // S5_GROUP_BLOCKS // LANES
S5_SLABS = SSM_FLAT // LANES


def _s5_kernel(u_ref, bre_ref, bim_ref, cre_ref, cim_ref, vec_ref, d_ref, wglu_ref, h0re_ref, h0im_ref,
               y_ref, hre_out, him_out, xre_ref, xim_ref, hre_s, him_s, *, n_seq, steps, carry):
    u = u_ref[...]
    ub = u.astype(BF16)
    cw = SSM_WIDTH // S5_GROUP_BLOCKS
    sw = SSM_FLAT // S5_GROUP_BLOCKS
    for j in range(S5_GROUP_BLOCKS):
        uj = ub[:, j * cw:(j + 1) * cw]
        bur = jnp.dot(uj, bre_ref[j], preferred_element_type=F32)
        bui = jnp.dot(uj, bim_ref[j], preferred_element_type=F32)
        zr = vec_ref[2:3, j * sw:(j + 1) * sw]
        zi = vec_ref[3:4, j * sw:(j + 1) * sw]
        xr = zr * bur - zi * bui
        xi = zr * bui + zi * bur
        for q in range(S5_BLOCK_LANES):
            xre_ref[j * S5_BLOCK_LANES + q] = xr[:, q * LANES:(q + 1) * LANES]
            xim_ref[j * S5_BLOCK_LANES + q] = xi[:, q * LANES:(q + 1) * LANES]

    if carry:
        @pl.when(pl.program_id(0) == 0)
        def _():
            hre_s[...] = h0re_ref[...]
            him_s[...] = h0im_ref[...]
    else:
        hre_s[...] = h0re_ref[...]
        him_s[...] = h0im_ref[...]

    for j in range(S5_GROUP_BLOCKS):
        slabs = [j * S5_BLOCK_LANES + q for q in range(S5_BLOCK_LANES)]
        lanes = [slice(s * LANES, (s + 1) * LANES) for s in slabs]
        ar = [jnp.broadcast_to(vec_ref[0:1, l], (n_seq, LANES)) for l in lanes]
        ai = [jnp.broadcast_to(vec_ref[1:2, l], (n_seq, LANES)) for l in lanes]

        def step(t, h, slabs=slabs, ar=ar, ai=ai):
            rows = pl.ds(t, n_seq, stride=steps) if n_seq > 1 else pl.ds(t, 1)
            out = []
            for q, s in enumerate(slabs):
                hr, hi = h[2 * q], h[2 * q + 1]
                nr = ar[q] * hr - ai[q] * hi + xre_ref[s, rows, :]
                ni = ar[q] * hi + ai[q] * hr + xim_ref[s, rows, :]
                xre_ref[s, rows, :] = nr
                xim_ref[s, rows, :] = ni
                out += [nr, ni]
            return tuple(out)

        h0 = []
        for l in lanes:
            h0 += [hre_s[:, l], him_s[:, l]]
        hT = lax.fori_loop(0, steps, step, tuple(h0), unroll=8)
        for q, l in enumerate(lanes):
            hre_s[:, l] = hT[2 * q]
            him_s[:, l] = hT[2 * q + 1]

    hre_out[...] = hre_s[...]
    him_out[...] = him_s[...]

    ys = []
    for j in range(S5_GROUP_BLOCKS):
        slabs = range(j * S5_BLOCK_LANES, (j + 1) * S5_BLOCK_LANES)
        hr = jnp.concatenate([xre_ref[s] for s in slabs], axis=-1).astype(BF16)
        hi = jnp.concatenate([xim_ref[s] for s in slabs], axis=-1).astype(BF16)
        ys.append(jnp.dot(hr, cre_ref[j], preferred_element_type=F32)
                  - jnp.dot(hi, cim_ref[j], preferred_element_type=F32))
    y = jnp.concatenate(ys, axis=-1) + d_ref[...] * u
    z = _gelu(y)
    gate = _sigmoid(jnp.dot(z.astype(BF16), wglu_ref[...], preferred_element_type=F32))
    y_ref[...] = (z * gate).astype(BF16)


def _s5_mixer(proj, row0, n_rows, n_seq, steps, carry, wts, h0_re, h0_im):
    bre, bim, cre, cim, vec, d, wglu = wts
    chunk = n_seq * steps
    n_chunks = n_rows // chunk
    blk0 = row0 // chunk
    st_rows = h0_re.shape[0]
    st_map = (lambda i: (0, 0)) if carry else (lambda i: (i, 0))
    fixed2 = lambda i: (0, 0)
    fixed3 = lambda i: (0, 0, 0)
    kern = functools.partial(_s5_kernel, n_seq=n_seq, steps=steps, carry=carry)
    return pl.pallas_call(
        kern,
        grid=(n_chunks,),
        in_specs=[pl.BlockSpec((chunk, SSM_WIDTH), lambda i: (blk0 + i, 0)),
                  pl.BlockSpec(bre.shape, fixed3), pl.BlockSpec(bim.shape, fixed3),
                  pl.BlockSpec(cre.shape, fixed3), pl.BlockSpec(cim.shape, fixed3),
                  pl.BlockSpec(vec.shape, fixed2), pl.BlockSpec(d.shape, fixed2),
                  pl.BlockSpec(wglu.shape, fixed2),
                  pl.BlockSpec((n_seq, SSM_FLAT), st_map), pl.BlockSpec((n_seq, SSM_FLAT), st_map)],
        out_specs=[pl.BlockSpec((chunk, SSM_WIDTH), lambda i: (i, 0)),
                   pl.BlockSpec((n_seq, SSM_FLAT), st_map), pl.BlockSpec((n_seq, SSM_FLAT), st_map)],
        out_shape=[jax.ShapeDtypeStruct((n_rows, SSM_WIDTH), BF16),
                   jax.ShapeDtypeStruct((st_rows, SSM_FLAT), F32),
                   jax.ShapeDtypeStruct((st_rows, SSM_FLAT), F32)],
        scratch_shapes=[pltpu.VMEM((S5_SLABS, chunk, LANES), F32), pltpu.VMEM((S5_SLABS, chunk, LANES), F32),
                        pltpu.VMEM((n_seq, SSM_FLAT), F32), pltpu.VMEM((n_seq, SSM_FLAT), F32)],
        compiler_params=_cparams("arbitrary"),
        name="s5_mixer_carry" if carry else "s5_mixer_batch",
    )(proj, bre, bim, cre, cim, vec, d, wglu, h0_re, h0_im)


def _s5_weights(a_re, a_im, log_dt, b_re, b_im, c_re, c_im, d, w_glu):
    dt = jnp.exp(log_dt)[:, None]
    mag = jnp.exp(a_re * dt)
    abar_re, abar_im = mag * jnp.cos(a_im * dt), mag * jnp.sin(a_im * dt)
    den = a_re * a_re + a_im * a_im
    zr = ((abar_re - 1.0) * a_re + abar_im * a_im) / den
    zi = (abar_im * a_re - (abar_re - 1.0) * a_im) / den
    flat = lambda v: v.reshape(1, SSM_FLAT)
    vec = jnp.concatenate([flat(abar_re), flat(abar_im), flat(zr), flat(zi),
                           jnp.zeros((SUBLANES - 4, SSM_FLAT), F32)], axis=0)
    gb = SSM_GROUPS // S5_GROUP_BLOCKS
    eye = jnp.eye(gb, dtype=F32)

    def pack_b(b):
        bb = b.reshape(S5_GROUP_BLOCKS, gb, SSM_STATE, SSM_GROUP_CH)
        m = jnp.einsum('jgph,gk->jghkp', bb, eye)
        return m.reshape(S5_GROUP_BLOCKS, gb * SSM_GROUP_CH, gb * SSM_STATE).astype(BF16)

    def pack_c(c):
        cc = c.reshape(S5_GROUP_BLOCKS, gb, SSM_GROUP_CH, SSM_STATE)
        m = jnp.einsum('jghp,gk->jgpkh', cc, eye)
        return m.reshape(S5_GROUP_BLOCKS, gb * SSM_STATE, gb * SSM_GROUP_CH).astype(BF16)

    return (pack_b(b_re), pack_b(b_im), pack_c(c_re), pack_c(c_im), vec, d.reshape(1, SSM_WIDTH),
            w_glu.astype(BF16))


POOL_HIST = 16


def _pool_seq_kernel(u_ref, buf_ref, w_ref, scale_ref, y_ref, new_ref, xc_ref, *, chunk, pos0):
    i = pl.program_id(0)

    @pl.when(i == 0)
    def _():
        xc_ref[0:POOL_HIST, :] = buf_ref[...]

    u = u_ref[...]
    xc_ref[POOL_HIST:POOL_HIST + chunk, :] = u
    pos = pos0 + i * chunk + lax.broadcasted_iota(jnp.int32, (chunk, POOL_CH), 0)
    outs = []
    for g, wd in enumerate(POOL_WINDOWS):
        lanes = slice(g * POOL_CH, (g + 1) * POOL_CH)
        ug = u[:, lanes]
        acc = ug
        for k in range(1, wd):
            acc = acc + xc_ref[pl.ds(POOL_HIST - k, chunk), lanes]
        mix = acc / jnp.minimum(wd, pos + 1).astype(F32) - ug
        outs.append(jnp.dot(mix.astype(BF16), w_ref[g], preferred_element_type=F32))
    y_ref[...] = (jnp.concatenate(outs, axis=-1) * scale_ref[...]).astype(BF16)
    tail = xc_ref[chunk:chunk + POOL_HIST, :]
    xc_ref[0:POOL_HIST, :] = tail
    new_ref[...] = tail


def _pool_mixer_seq(proj, row0, n_rows, pos0, buf16, w, scale, chunk=256):
    blk0 = row0 // chunk
    kern = functools.partial(_pool_seq_kernel, chunk=chunk, pos0=pos0)
    return pl.pallas_call(
        kern,
        grid=(n_rows // chunk,),
        in_specs=[pl.BlockSpec((chunk, POOL_WIDTH), lambda i: (blk0 + i, 1)),
                  pl.BlockSpec((POOL_HIST, POOL_WIDTH), lambda i: (0, 0)),
                  pl.BlockSpec(w.shape, lambda i: (0, 0, 0)),
                  pl.BlockSpec((1, POOL_WIDTH), lambda i: (0, 0))],
        out_specs=[pl.BlockSpec((chunk, POOL_WIDTH), lambda i: (i, 0)),
                   pl.BlockSpec((POOL_HIST, POOL_WIDTH), lambda i: (0, 0))],
        out_shape=[jax.ShapeDtypeStruct((n_rows, POOL_WIDTH), BF16),
                   jax.ShapeDtypeStruct((POOL_HIST, POOL_WIDTH), F32)],
        scratch_shapes=[pltpu.VMEM((POOL_HIST + chunk, POOL_WIDTH), F32)],
        compiler_params=_cparams("arbitrary"),
        name="pool_mixer_seq",
    )(proj, buf16, w, scale)


def _pool_batch_kernel(u_ref, buf_ref, w_ref, scale_ref, y_ref, new_ref, *, steps, pos0):
    def xrow(j, lanes):
        return buf_ref[j, :, lanes] if j < POOL_BUF else u_ref[j - POOL_BUF, :, lanes]

    nb = u_ref.shape[1]
    for g, wd in enumerate(POOL_WINDOWS):
        lanes = slice(g * POOL_CH, (g + 1) * POOL_CH)
        mixes = []
        for t in range(steps):
            ug = u_ref[t, :, lanes]
            acc = ug
            for k in range(1, wd):
                acc = acc + xrow(POOL_BUF + t - k, lanes)
            mixes.append(acc / float(min(wd, pos0 + t + 1)) - ug)
        mix = jnp.concatenate(mixes, axis=0).astype(BF16)
        yg = jnp.dot(mix, w_ref[g], preferred_element_type=F32) * scale_ref[:, lanes]
        for t in range(steps):
            y_ref[t, :, lanes] = yg[t * nb:(t + 1) * nb].astype(BF16)
    for j in range(POOL_BUF):
        new_ref[j] = xrow(steps + j, slice(None))


def _pool_mixer_batch(u_t, buf_t, pos0, w, scale):
    steps, nb, _ = u_t.shape
    kern = functools.partial(_pool_batch_kernel, steps=steps, pos0=pos0)
    full3 = lambda i: (0, 0, 0)
    return pl.pallas_call(
        kern,
        grid=(1,),
        in_specs=[pl.BlockSpec(u_t.shape, full3), pl.BlockSpec(buf_t.shape, full3),
                  pl.BlockSpec(w.shape, full3), pl.BlockSpec((1, POOL_WIDTH), lambda i: (0, 0))],
        out_specs=[pl.BlockSpec(u_t.shape, full3), pl.BlockSpec(buf_t.shape, full3)],
        out_shape=[jax.ShapeDtypeStruct(u_t.shape, BF16), jax.ShapeDtypeStruct(buf_t.shape, F32)],
        compiler_params=_cparams("arbitrary"),
        name="pool_mixer_batch",
    )(u_t, buf_t, w, scale)


CHUNK_FLAT = CMP_STRIDE * KV_WIDTH
CMP_HID_ALL = N_KV * CMP_HIDDEN


def _compress_kernel(*refs, n_pages):
    if n_pages > 1:
        refs = refs[1:]
    page_refs = refs[:n_pages]
    wbig_ref, pe_ref, w1_ref, w2_ref, out_ref, b_ref = refs[n_pages:]
    chunks = jnp.concatenate([r[...] for r in page_refs], axis=0) if n_pages > 1 else page_refs[0][...]
    n = chunks.shape[0]
    ab = jnp.dot(chunks.astype(BF16), wbig_ref[...], preferred_element_type=F32)
    b_ref[0:n, :] = ab[:, CMP_HID_ALL:]
    b_ref[n:n + SUBLANES, :] = jnp.zeros((SUBLANES, CMP_HID_ALL), F32)
    hid0 = jnp.dot(pe_ref[...], w1_ref[...], preferred_element_type=F32)[0:1]
    hid = ab[:, :CMP_HID_ALL] + b_ref[pl.ds(1, n), :] + jnp.concatenate([hid0] * N_KV, axis=-1)
    out = jnp.dot(_gelu(hid).astype(BF16), w2_ref[...], preferred_element_type=F32)
    row = lax.broadcasted_iota(jnp.int32, out.shape, 0)
    out_ref[...] = jnp.where(row < n - 1, out, 0.0).astype(out_ref.dtype)


def _compress_weights(pe, w1, w2):
    w1b = w1.reshape(2, CMP_STRIDE, HEAD_DIM, CMP_HIDDEN)
    eye = jnp.eye(N_KV, dtype=F32)
    big = jnp.einsum('msdh,kj->mskdjh', w1b, eye).reshape(2, CHUNK_FLAT, CMP_HID_ALL)
    wbig = jnp.concatenate([big[0], big[1]], axis=-1).astype(BF16)
    w2big = jnp.einsum('hd,kj->khjd', w2, eye).reshape(CMP_HID_ALL, KV_WIDTH).astype(BF16)
    pe8 = jnp.concatenate([pe.reshape(1, CMP_BLOCK * HEAD_DIM),
                           jnp.zeros((SUBLANES - 1, CMP_BLOCK * HEAD_DIM), F32)], axis=0).astype(BF16)
    return wbig, pe8, w1.astype(BF16), w2big


def _compress_seq(chunks, wts):
    n = chunks.shape[0]
    wbig, pe8, w1, w2big = wts
    full = lambda i: (0, 0)
    return pl.pallas_call(
        functools.partial(_compress_kernel, n_pages=1),
        grid=(1,),
        in_specs=[pl.BlockSpec(chunks.shape, full), pl.BlockSpec(wbig.shape, full), pl.BlockSpec(pe8.shape, full),
                  pl.BlockSpec(w1.shape, full), pl.BlockSpec(w2big.shape, full)],
        out_specs=pl.BlockSpec((n, KV_WIDTH), full),
        out_shape=jax.ShapeDtypeStruct((n, KV_WIDTH), BF16),
        scratch_shapes=[pltpu.VMEM((n + SUBLANES, CMP_HID_ALL), F32)],
        compiler_params=_cparams("arbitrary"),
        name="compress_seq",
    )(chunks, wbig, pe8, w1, w2big)


def _compress_paged(cache, layer, page_table, wts):
    nb, n_pages = page_table.shape
    cpp = PAGE_SIZE // CMP_STRIDE
    n = n_pages * cpp
    wbig, pe8, w1, w2big = wts
    full = lambda b, pt: (0, 0)

    def page_spec(p):
        return pl.BlockSpec((None, None, cpp, CHUNK_FLAT), lambda b, pt: (layer, pt[b, p], 0, 0))

    grid_spec = pltpu.PrefetchScalarGridSpec(
        num_scalar_prefetch=1,
        grid=(nb,),
        in_specs=[page_spec(p) for p in range(n_pages)]
        + [pl.BlockSpec(wbig.shape, full), pl.BlockSpec(pe8.shape, full), pl.BlockSpec(w1.shape, full),
           pl.BlockSpec(w2big.shape, full)],
        out_specs=pl.BlockSpec((None, n, KV_WIDTH), lambda b, pt: (b, 0, 0)),
        scratch_shapes=[pltpu.VMEM((n + SUBLANES, CMP_HID_ALL), F32)],
    )
    return pl.pallas_call(
        functools.partial(_compress_kernel, n_pages=n_pages),
        grid_spec=grid_spec,
        out_shape=jax.ShapeDtypeStruct((nb, n, KV_WIDTH), BF16),
        compiler_params=_cparams("arbitrary"),
        name="compress_paged",
    )(page_table, *([cache] * n_pages), wbig, pe8, w1, w2big)


def _alibi_slope(h):
    return 2.0 ** (-8.0 * (h + 1) / N_HEADS)


NT_DIMS = (((1,), (1,)), ((), ()))
QROWS = GQA * Q_BLOCK
KEY_TILE = 128
WIN_TILES = WINDOW // KEY_TILE


def _nsa_seq_kernel(q_ref, gate_ref, kc_ref, vct_ref, ks_ref, vst_ref, kw_ref, vwt_ref, o_ref,
                    qg_ref, selb_ref, m_ref, l_ref, acc_ref, oc_ref, os_ref, ow_ref, *, n_cmp_pad, n_sel):
    i = pl.program_id(0)
    base = i * Q_BLOCK

    qs = q_ref[...] * (HEAD_DIM ** -0.5)
    low_half = lax.broadcasted_iota(jnp.int32, (Q_BLOCK, LANES), 1) < HEAD_DIM
    for k in range(N_KV):
        keep = low_half if k % 2 == 0 else jnp.logical_not(low_half)
        parts = []
        for g in range(GQA):
            h = GQA * k + g
            pair = qs[:, (h // 2) * LANES:(h // 2 + 1) * LANES]
            if h % 2 != k % 2:
                pair = pltpu.roll(pair, HEAD_DIM, 1)
            parts.append(jnp.where(keep, pair, 0.0).astype(BF16))
        qg_ref[k] = jnp.concatenate(parts, axis=0)

    lane = lax.broadcasted_iota(jnp.int32, (1, QROWS), 1)
    qq_i = lane % Q_BLOCK
    qq = qq_i.astype(F32)
    g_lane = lane // Q_BLOCK

    def slope_row(k):
        r = jnp.full((1, QROWS), _alibi_slope(GQA * k + GQA - 1), F32)
        for g in range(GQA - 1):
            r = jnp.where(g_lane == g, _alibi_slope(GQA * k + g), r)
        return r

    slopes = [slope_row(k) for k in range(N_KV)]

    n_io = lax.broadcasted_iota(jnp.int32, (n_cmp_pad, QROWS), 0)
    distc = (base.astype(F32) + qq) - (n_io.astype(F32) * CMP_STRIDE + (CMP_BLOCK - 1) / 2.0)
    visc = (n_io * CMP_STRIDE + (CMP_BLOCK - 1)) <= (base + qq_i)
    viscf = visc.astype(F32)
    sj_o = lax.broadcasted_iota(jnp.int32, (n_sel, n_cmp_pad), 0)
    nn_o = lax.broadcasted_iota(jnp.int32, (n_sel, n_cmp_pad), 1)
    ovt = jnp.where((nn_o * CMP_STRIDE < (sj_o + 1) * SEL_BLOCK)
                    & (nn_o * CMP_STRIDE + (CMP_BLOCK - 1) >= sj_o * SEL_BLOCK), 1.0, 0.0).astype(BF16)
    sjf = lax.broadcasted_iota(jnp.int32, (n_sel, Q_BLOCK), 0).astype(F32)
    curf = ((base + lax.broadcasted_iota(jnp.int32, (1, Q_BLOCK), 1)) // SEL_BLOCK).astype(F32)
    visible = sjf <= curf
    forced = visible & ((sjf == 0.0) | (sjf == curf) | (sjf == curf - 1.0))

    for k in range(N_KV):
        kl = slice((k // 2) * LANES, (k // 2 + 1) * LANES)
        vr = slice(k * HEAD_DIM, (k + 1) * HEAD_DIM)
        s = lax.dot_general(kc_ref[:, kl], qg_ref[k], NT_DIMS, preferred_element_type=F32)
        s = jnp.where(visc, s - slopes[k] * distc, NEG_INF)
        m = jnp.max(s, axis=0, keepdims=True)
        p = jnp.exp(s - m) * viscf
        p = p * (1.0 / jnp.maximum(jnp.sum(p, axis=0, keepdims=True), TINY))
        oc_ref[k] = jnp.dot(vct_ref[vr, :], p.astype(BF16), preferred_element_type=F32)
        psum = p[:, 0:Q_BLOCK]
        for g in range(1, GQA):
            psum = psum + p[:, g * Q_BLOCK:(g + 1) * Q_BLOCK]
        p_hi = psum.astype(BF16)
        p_lo = (psum - p_hi.astype(F32)).astype(BF16)
        imp = (jnp.dot(ovt, p_hi, preferred_element_type=F32)
               + jnp.dot(ovt, p_lo, preferred_element_type=F32))
        score = jnp.where(forced, FORCED_SCORE, jnp.where(visible, imp, -1.0))
        sel = jnp.zeros_like(score)
        for _ in range(min(TOP_N, n_sel)):
            mx = jnp.max(score, axis=0, keepdims=True)
            idx = jnp.min(jnp.where(score == mx, sjf, 1e9), axis=0, keepdims=True)
            hit = sjf == idx
            sel = jnp.where(hit & (mx >= 0.0), 1.0, sel)
            score = jnp.where(hit, -2.0, score)
        selb_ref[k] = jnp.where(sel > 0.0, 0.0, NEG_INF)

    dist0 = qq - lax.broadcasted_iota(jnp.int32, (KEY_TILE, QROWS), 0).astype(F32)

    def flash(k_ref, vt_ref, j_lo, use_sel, out_ref):
        m_ref[...] = jnp.full(m_ref.shape, NEG_INF, F32)
        l_ref[...] = jnp.zeros(l_ref.shape, F32)
        acc_ref[...] = jnp.zeros(acc_ref.shape, F32)

        def body(j, carry):
            dist = dist0 + ((i - j) * KEY_TILE).astype(F32)
            ok = dist >= 0.0
            if not use_sel:
                ok = ok & (dist < float(WINDOW))
            for k in range(N_KV):
                kl = slice((k // 2) * LANES, (k // 2 + 1) * LANES)
                vr = slice(k * HEAD_DIM, (k + 1) * HEAD_DIM)
                s = lax.dot_general(k_ref[j, :, kl], qg_ref[k], NT_DIMS, preferred_element_type=F32)
                s = s - slopes[k] * dist
                if use_sel:
                    b0 = jnp.broadcast_to(selb_ref[k, pl.ds(2 * j, 1), :], (SEL_BLOCK, Q_BLOCK))
                    b1 = jnp.broadcast_to(selb_ref[k, pl.ds(2 * j + 1, 1), :], (SEL_BLOCK, Q_BLOCK))
                    bias = jnp.concatenate([b0, b1], axis=0)
                    s = s + jnp.concatenate([bias] * GQA, axis=1)
                s = jnp.where(ok, s, NEG_INF)
                m_old = m_ref[k]
                m_new = jnp.maximum(m_old, jnp.max(s, axis=0, keepdims=True))
                alpha = jnp.exp(m_old - m_new)
                p = jnp.exp(s - m_new)
                l_ref[k] = alpha * l_ref[k] + jnp.sum(p, axis=0, keepdims=True)
                acc_ref[k] = alpha * acc_ref[k] + jnp.dot(vt_ref[j, vr, :], p.astype(BF16),
                                                          preferred_element_type=F32)
                m_ref[k] = m_new
            return carry

        lax.fori_loop(j_lo, i + 1, body, 0)
        for k in range(N_KV):
            out_ref[k] = acc_ref[k] * (1.0 / jnp.maximum(l_ref[k], TINY))

    flash(ks_ref, vst_ref, 0, True, os_ref)
    flash(kw_ref, vwt_ref, jnp.maximum(i - WIN_TILES, 0), False, ow_ref)

    gate = _sigmoid(gate_ref[...])
    blocks = []
    for k in range(N_KV):
        for g in range(GQA):
            h = GQA * k + g
            sl = slice(g * Q_BLOCK, (g + 1) * Q_BLOCK)
            blocks.append(oc_ref[k, :, sl] * gate[3 * h:3 * h + 1]
                          + os_ref[k, :, sl] * gate[3 * h + 1:3 * h + 2]
                          + ow_ref[k, :, sl] * gate[3 * h + 2:3 * h + 3])
    o_ref[...] = jnp.concatenate(blocks, axis=0).T.astype(BF16)


def _nsa_seq(proj, gate_t, kc, vct, ks_t, vst_t, kw_t, vwt_t, n_rows):
    n_tiles = n_rows // Q_BLOCK
    n_cmp_pad = kc.shape[0]
    n_sel = n_rows // SEL_BLOCK
    kern = functools.partial(_nsa_seq_kernel, n_cmp_pad=n_cmp_pad, n_sel=n_sel)
    c2 = lambda i: (0, 0)
    c3 = lambda i: (0, 0, 0)
    st = (N_KV, HEAD_DIM, QROWS)
    return pl.pallas_call(
        kern,
        grid=(n_tiles,),
        in_specs=[pl.BlockSpec((Q_BLOCK, NSA_WIDTH), lambda i: (i, 1)),
                  pl.BlockSpec((3 * N_HEADS, Q_BLOCK), lambda i: (0, i)),
                  pl.BlockSpec(kc.shape, c2), pl.BlockSpec(vct.shape, c2),
                  pl.BlockSpec(ks_t.shape, c3), pl.BlockSpec(vst_t.shape, c3),
                  pl.BlockSpec(kw_t.shape, c3), pl.BlockSpec(vwt_t.shape, c3)],
        out_specs=pl.BlockSpec((Q_BLOCK, NSA_WIDTH), lambda i: (i, 0)),
        out_shape=jax.ShapeDtypeStruct((n_rows, NSA_WIDTH), BF16),
        scratch_shapes=[pltpu.VMEM((N_KV, QROWS, LANES), BF16),
                        pltpu.VMEM((N_KV, n_sel, Q_BLOCK), F32),
                        pltpu.VMEM((N_KV, 1, QROWS), F32), pltpu.VMEM((N_KV, 1, QROWS), F32),
                        pltpu.VMEM(st, F32), pltpu.VMEM(st, F32), pltpu.VMEM(st, F32), pltpu.VMEM(st, F32)],
        compiler_params=_cparams("arbitrary"),
        name="nsa_seq",
    )(proj, gate_t, kc, vct, ks_t, vst_t, kw_t, vwt_t)


def _nsa_batch_kernel(pt_ref, *refs, n_pages, steps, pos0):
    del pt_ref
    ks_pages = refs[:n_pages]
    vs_pages = refs[n_pages:2 * n_pages]
    (kwin_ref, vwin_ref, kc_ref, vc_ref, q_ref, ksn_ref, vsn_ref, kwn_ref, vwn_ref, gate_ref,
     o_ref, kwo_ref, vwo_ref, bsel_ref, bwin_ref, bcmp_ref, e_ref, ov_ref) = refs[2 * n_pages:]
    rows = N_HEADS * steps
    past = n_pages * PAGE_SIZE
    nk = past + KEY_TILE
    nw = WINDOW + KEY_TILE

    @pl.when(pl.program_id(0) == 0)
    def _():
        def tables(width):
            r = lax.broadcasted_iota(jnp.int32, (rows, width), 0)
            c = lax.broadcasted_iota(jnp.int32, (rows, width), 1)
            slope = jnp.exp((-8.0 * math.log(2.0) / N_HEADS) * (r // steps + 1).astype(F32))
            return slope, pos0 + r % steps, c

        slope, qpos, key = tables(nk)
        bsel_ref[...] = jnp.where(key <= qpos, -slope * (qpos - key).astype(F32), NEG_INF)
        slope, qpos, w = tables(nw)
        kpos = pos0 - WINDOW + w
        d = qpos - kpos
        bwin_ref[...] = jnp.where((d >= 0) & (d < WINDOW) & (kpos >= 0), -slope * d.astype(F32), NEG_INF)
        slope, qpos, n = tables(LANES)
        c_mid = n.astype(F32) * CMP_STRIDE + (CMP_BLOCK - 1) / 2.0
        bcmp_ref[...] = jnp.where(n * CMP_STRIDE + (CMP_BLOCK - 1) <= qpos,
                                  -slope * (qpos.astype(F32) - c_mid), NEG_INF)
        sj = lax.broadcasted_iota(jnp.int32, (LANES, nk), 0)
        key = lax.broadcasted_iota(jnp.int32, (LANES, nk), 1)
        e_ref[...] = jnp.where(key // SEL_BLOCK == sj, 1.0, 0.0).astype(BF16)
        n = lax.broadcasted_iota(jnp.int32, (LANES, LANES), 0)
        sj = lax.broadcasted_iota(jnp.int32, (LANES, LANES), 1)
        ov_ref[...] = jnp.where((n * CMP_STRIDE < (sj + 1) * SEL_BLOCK)
                                & (n * CMP_STRIDE + (CMP_BLOCK - 1) >= sj * SEL_BLOCK), 1.0, 0.0).astype(BF16)

    qs = q_ref[...] * (HEAD_DIM ** -0.5)
    low_half = lax.broadcasted_iota(jnp.int32, (steps, LANES), 1) < HEAD_DIM
    zero = jnp.zeros((steps, LANES), F32)
    pieces = []
    for k in range(N_KV):
        keep = low_half if k % 2 == 0 else jnp.logical_not(low_half)
        for g in range(GQA):
            h = GQA * k + g
            pair = qs[:, (h // 2) * LANES:(h // 2 + 1) * LANES]
            if h % 2 != k % 2:
                pair = pltpu.roll(pair, HEAD_DIM, 1)
            blk = jnp.where(keep, pair, 0.0)
            pieces.append(jnp.concatenate([blk, zero] if k // 2 == 0 else [zero, blk], axis=1))
    qb = jnp.concatenate(pieces, axis=0).astype(BF16)

    def new_tile(ref):
        return jnp.concatenate([ref[...], jnp.zeros((KEY_TILE - steps, KV_WIDTH), F32)], axis=0).astype(BF16)

    def softmax_pv(s, v_parts):
        m = jnp.max(s, axis=-1, keepdims=True)
        p = jnp.exp(s - m)
        l = jnp.sum(p, axis=-1, keepdims=True)
        acc = None
        off = 0
        for v in v_parts:
            part = jnp.dot(p[:, off:off + v.shape[0]].astype(BF16), v, preferred_element_type=F32)
            acc = part if acc is None else acc + part
            off += v.shape[0]
        return acc * (1.0 / jnp.maximum(l, TINY))

    bc = bcmp_ref[...]
    s = lax.dot_general(qb, kc_ref[...], NT_DIMS, preferred_element_type=F32) + bc
    m = jnp.max(s, axis=-1, keepdims=True)
    p = jnp.exp(s - m) * jnp.where(bc > 0.5 * NEG_INF, 1.0, 0.0)
    p = p * (1.0 / jnp.maximum(jnp.sum(p, axis=-1, keepdims=True), TINY))
    o_c = jnp.dot(p.astype(BF16), vc_ref[...], preferred_element_type=F32)

    grp = GQA * steps
    psum = []
    for k in range(N_KV):
        acc = p[k * grp:k * grp + steps]
        for g in range(1, GQA):
            acc = acc + p[k * grp + g * steps:k * grp + (g + 1) * steps]
        psum.append(acc)
    psum = jnp.concatenate(psum, axis=0)
    p_hi = psum.astype(BF16)
    p_lo = (psum - p_hi.astype(F32)).astype(BF16)
    imp = (jnp.dot(p_hi, ov_ref[...], preferred_element_type=F32)
           + jnp.dot(p_lo, ov_ref[...], preferred_element_type=F32))
    sjf = lax.broadcasted_iota(jnp.int32, imp.shape, 1).astype(F32)
    step_of_row = lax.broadcasted_iota(jnp.int32, imp.shape, 0) % steps
    curf = ((pos0 + step_of_row) // SEL_BLOCK).astype(F32)
    visible = sjf <= curf
    forced = visible & ((sjf == 0.0) | (sjf == curf) | (sjf == curf - 1.0))
    score = jnp.where(forced, FORCED_SCORE, jnp.where(visible, imp, -1.0))
    sel = jnp.zeros_like(score)
    for _ in range(TOP_N):
        mx = jnp.max(score, axis=-1, keepdims=True)
        idx = jnp.min(jnp.where(score == mx, sjf, 1e9), axis=-1, keepdims=True)
        hit = sjf == idx
        sel = jnp.where(hit & (mx >= 0.0), 1.0, sel)
        score = jnp.where(hit, -2.0, score)
    selk = jnp.dot(sel.astype(BF16), e_ref[...], preferred_element_type=F32)
    selb = jnp.where(selk > 0.5, 0.0, NEG_INF)
    sb = jnp.concatenate([selb[k * steps:(k + 1) * steps] for k in range(N_KV) for _ in range(GQA)], axis=0)

    k_all = jnp.concatenate([r[...] for r in ks_pages], axis=0).astype(BF16)
    v_all = jnp.concatenate([r[...] for r in vs_pages], axis=0).astype(BF16)
    s = jnp.concatenate([lax.dot_general(qb, k_all, NT_DIMS, preferred_element_type=F32),
                         lax.dot_general(qb, new_tile(ksn_ref), NT_DIMS, preferred_element_type=F32)], axis=1)
    o_s = softmax_pv(s + bsel_ref[...] + sb, [v_all, new_tile(vsn_ref)])

    s = jnp.concatenate([lax.dot_general(qb, kwin_ref[...].astype(BF16), NT_DIMS, preferred_element_type=F32),
                         lax.dot_general(qb, new_tile(kwn_ref), NT_DIMS, preferred_element_type=F32)], axis=1)
    o_w = softmax_pv(s + bwin_ref[...], [vwin_ref[...].astype(BF16), new_tile(vwn_ref)])

    for src, nw_ref, dst in ((kwin_ref, kwn_ref, kwo_ref), (vwin_ref, vwn_ref, vwo_ref)):
        dst[0:WINDOW - steps, :] = src[steps:WINDOW, :]
        dst[WINDOW - steps:WINDOW, :] = nw_ref[...]

    gates = _sigmoid(gate_ref[...])
    y = o_c * gates[:, 0:1] + o_s * gates[:, 1:2] + o_w * gates[:, 2:3]
    for h in range(N_HEADS):
        k = h // GQA
        o_ref[:, h * HEAD_DIM:(h + 1) * HEAD_DIM] = y[h * steps:(h + 1) * steps, k * HEAD_DIM:(k + 1) * HEAD_DIM]


def _nsa_batch(proj, row0, steps, pos0, gate_rows, kc, vc, ks_cache, vs_cache, layer, page_table, kwin, vwin):
    nb, n_pages = page_table.shape
    rows = N_HEADS * steps
    nk = n_pages * PAGE_SIZE + KEY_TILE
    blk0 = row0 // steps
    kern = functools.partial(_nsa_batch_kernel, n_pages=n_pages, steps=steps, pos0=pos0)

    def page_spec(p):
        return pl.BlockSpec((None, None, PAGE_SIZE, KV_WIDTH), lambda b, pt: (layer, pt[b, p], 0, 0))

    per_b = lambda b, pt: (b, 0, 0)
    kv_col0 = sum(PROJ_SIZES[:3]) // KV_WIDTH

    def new_spec(j):
        return pl.BlockSpec((steps, KV_WIDTH), lambda b, pt: (blk0 + b, kv_col0 + j))

    grid_spec = pltpu.PrefetchScalarGridSpec(
        num_scalar_prefetch=1,
        grid=(nb,),
        in_specs=[page_spec(p) for p in range(n_pages)] + [page_spec(p) for p in range(n_pages)]
        + [pl.BlockSpec((None, WINDOW, KV_WIDTH), per_b), pl.BlockSpec((None, WINDOW, KV_WIDTH), per_b),
           pl.BlockSpec((None,) + kc.shape[1:], per_b), pl.BlockSpec((None,) + vc.shape[1:], per_b),
           pl.BlockSpec((steps, NSA_WIDTH), lambda b, pt: (blk0 + b, 1)),
           new_spec(2), new_spec(3), new_spec(4), new_spec(5),
           pl.BlockSpec((None, rows, SUBLANES), per_b)],
        out_specs=[pl.BlockSpec((steps, NSA_WIDTH), lambda b, pt: (b, 0)),
                   pl.BlockSpec((None, WINDOW, KV_WIDTH), per_b), pl.BlockSpec((None, WINDOW, KV_WIDTH), per_b)],
        scratch_shapes=[pltpu.VMEM((rows, nk), F32), pltpu.VMEM((rows, WINDOW + KEY_TILE), F32),
                        pltpu.VMEM((rows, LANES), F32), pltpu.VMEM((LANES, nk), BF16),
                        pltpu.VMEM((LANES, LANES), BF16)],
    )
    return pl.pallas_call(
        kern,
        grid_spec=grid_spec,
        out_shape=[jax.ShapeDtypeStruct((nb * steps, NSA_WIDTH), F32),
                   jax.ShapeDtypeStruct((nb, WINDOW, KV_WIDTH), F32),
                   jax.ShapeDtypeStruct((nb, WINDOW, KV_WIDTH), F32)],
        compiler_params=_cparams("arbitrary"),
        name="nsa_batch",
    )(page_table, *([ks_cache] * n_pages), *([vs_cache] * n_pages), kwin, vwin, kc, vc,
      proj, proj, proj, proj, proj, gate_rows)


S5_SEQ_CHUNK = 256
S5_BATCH_SEQS = 16


def kernel(x_prompt, x_sample, cache_k_cmp, cache_v_cmp, cache_k_sel, cache_v_sel, cache_k_win, cache_v_win, state_ssm_re, state_ssm_im, state_pool, page_table, w_in, ssm_a_re, ssm_a_im, ssm_log_dt, ssm_b_re, ssm_b_im, ssm_c_re, ssm_c_im, ssm_d, ssm_w_glu, pool_w, pool_scale, cmp_pe, cmp_w1, cmp_w2, w_out, ln1_g, ln1_b, mlp_w1, mlp_w2, ln2_g, ln2_b):
    bp, lp, d = x_prompt.shape
    nb, steps, _ = x_sample.shape
    assert bp == 1 and d == D_MODEL and steps < CMP_STRIDE
    n_p, n_s = bp * lp, nb * steps
    n_phys = cache_k_cmp.shape[1]
    past_len = page_table.shape[1] * PAGE_SIZE
    assert cache_k_win.shape[2] == WINDOW and past_len >= WINDOW

    x = jnp.concatenate([x_prompt.reshape(n_p, d), x_sample.reshape(n_s, d)], axis=0)
    cpp = PAGE_SIZE // CMP_STRIDE
    kcmp_pages = cache_k_cmp.reshape(DEPTH, n_phys, cpp, CHUNK_FLAT)
    vcmp_pages = cache_v_cmp.reshape(DEPTH, n_phys, cpp, CHUNK_FLAT)
    ksel_pages = cache_k_sel.reshape(DEPTH, n_phys, PAGE_SIZE, KV_WIDTH)
    vsel_pages = cache_v_sel.reshape(DEPTH, n_phys, PAGE_SIZE, KV_WIDTH)
    kv_col0 = sum(PROJ_SIZES[:3])
    zero_state = jnp.zeros((1, SSM_FLAT), F32)
    zero_buf = jnp.zeros((POOL_HIST, POOL_WIDTH), F32)
    row = lambda v: v.reshape(1, -1)

    new_p, new_s = [], []
    for l in range(DEPTH):
        w_in_l = jnp.pad(w_in[l], ((0, 0), (0, PROJ_PAD - PROJ_WIDTH))).astype(BF16)
        proj = _input_projection(x, w_in_l)
        kv = [proj[:, kv_col0 + j * KV_WIDTH:kv_col0 + (j + 1) * KV_WIDTH] for j in range(6)]
        gate_logits = proj[:, GATE_COL:GATE_COL + 3 * N_HEADS]

        s5w = _s5_weights(ssm_a_re[l], ssm_a_im[l], ssm_log_dt[l], ssm_b_re[l], ssm_b_im[l], ssm_c_re[l],
                          ssm_c_im[l], ssm_d[l], ssm_w_glu[l])
        ys_p, hr_p, hi_p = _s5_mixer(proj, 0, n_p, 1, S5_SEQ_CHUNK, True, s5w, zero_state, zero_state)
        ys_s, hr_s, hi_s = _s5_mixer(proj, n_p, n_s, S5_BATCH_SEQS, steps, False, s5w,
                                     state_ssm_re[l].reshape(nb, SSM_FLAT), state_ssm_im[l].reshape(nb, SSM_FLAT))

        pw, psc = pool_w[l].astype(BF16), row(pool_scale[l])
        yp_p, pool_p = _pool_mixer_seq(proj, 0, n_p, 0, zero_buf, pw, psc)
        u_t = proj[n_p:, SSM_WIDTH:SSM_WIDTH + POOL_WIDTH].reshape(nb, steps, POOL_WIDTH).transpose(1, 0, 2)
        yp_s_t, pool_s_t = _pool_mixer_batch(u_t, state_pool[l].transpose(1, 0, 2), past_len, pw, psc)
        yp_s = yp_s_t.transpose(1, 0, 2).reshape(n_s, POOL_WIDTH)

        cw_k = _compress_weights(cmp_pe[l, 0], cmp_w1[l, 0], cmp_w2[l, 0])
        cw_v = _compress_weights(cmp_pe[l, 1], cmp_w1[l, 1], cmp_w2[l, 1])
        kc_p = _compress_seq(kv[0][:n_p].reshape(n_p // CMP_STRIDE, CHUNK_FLAT), cw_k)
        vc_p = _compress_seq(kv[1][:n_p].reshape(n_p // CMP_STRIDE, CHUNK_FLAT), cw_v)
        tiles = lambda a: a[:n_p].astype(BF16).reshape(n_p // KEY_TILE, KEY_TILE, KV_WIDTH)
        tiles_t = lambda a: tiles(a).transpose(0, 2, 1)
        yn_p = _nsa_seq(proj, gate_logits[:n_p].T, kc_p, vc_p.T, tiles(kv[2]), tiles_t(kv[3]), tiles(kv[4]),
                        tiles_t(kv[5]), n_p)

        kc_s = _compress_paged(kcmp_pages, l, page_table, cw_k)
        vc_s = _compress_paged(vcmp_pages, l, page_table, cw_v)
        gate_rows = gate_logits[n_p:].reshape(nb, steps, N_HEADS, 3).transpose(0, 2, 1, 3)
        gate_rows = jnp.pad(gate_rows.reshape(nb, N_HEADS * steps, 3), ((0, 0), (0, 0), (0, SUBLANES - 3)))
        yn_s, kw_s, vw_s = _nsa_batch(proj, n_p, steps, past_len, gate_rows, kc_s, vc_s, ksel_pages, vsel_pages, l,
                                      page_table, cache_k_win[l].reshape(nb, WINDOW, KV_WIDTH),
                                      cache_v_win[l].reshape(nb, WINDOW, KV_WIDTH))

        y_ssm = jnp.concatenate([ys_p, ys_s], axis=0)
        y_pool = jnp.concatenate([yp_p, yp_s], axis=0)
        y_nsa = jnp.concatenate([yn_p, yn_s.astype(BF16)], axis=0)
        x = _output_projection_ln(x, y_ssm, y_pool, y_nsa, w_out[l].astype(BF16), row(ln1_g[l]), row(ln1_b[l]))
        x = _mlp_ln(x, mlp_w1[l].astype(BF16), mlp_w2[l].astype(BF16), row(ln2_g[l]), row(ln2_b[l]))

        heads = lambda a, b_, t: a.reshape(b_, t, N_KV, HEAD_DIM)
        n_keep = min(WINDOW, lp)
        new_p.append([heads(kv[j][:n_p], bp, lp) for j in range(4)]
                     + [heads(kv[j][n_p - n_keep:n_p], bp, n_keep) for j in (4, 5)]
                     + [hr_p.reshape(bp, SSM_GROUPS, SSM_STATE), hi_p.reshape(bp, SSM_GROUPS, SSM_STATE),
                        pool_p[POOL_HIST - POOL_BUF:].reshape(bp, POOL_BUF, POOL_WIDTH)])
        new_s.append([heads(kv[j][n_p:], nb, steps) for j in range(4)]
                     + [heads(kw_s, nb, WINDOW), heads(vw_s, nb, WINDOW)]
                     + [hr_s.reshape(nb, SSM_GROUPS, SSM_STATE), hi_s.reshape(nb, SSM_GROUPS, SSM_STATE),
                        pool_s_t.transpose(1, 0, 2)])

    st_p = [jnp.stack(f) for f in zip(*new_p)]
    st_s = [jnp.stack(f) for f in zip(*new_s)]
    out = [x[:n_p].reshape(bp, lp, d), x[n_p:].reshape(nb, steps, d)]
    for a, b_ in zip(st_p, st_s):
        out += [a, b_]
    return tuple(out)
```

```python
import functools
import math

import jax
import jax.numpy as jnp
from jax import lax
from jax.experimental import pallas as pl
from jax.experimental.pallas import tpu as pltpu

F32 = jnp.float32
BF16 = jnp.bfloat16

D_MODEL = 2048
DEPTH = 2
PAGE_SIZE = 128
SSM_WIDTH = 512
SSM_GROUP_CH = 16
SSM_GROUPS = 32
SSM_STATE = 64
SSM_FLAT = SSM_GROUPS * SSM_STATE
POOL_WIDTH = 512
POOL_WINDOWS = (2, 4, 8, 16)
POOL_CH = 128
POOL_BUF = 15
NSA_WIDTH = 1024
HEAD_DIM = 64
N_HEADS = 16
N_KV = 4
GQA = 4
KV_WIDTH = N_KV * HEAD_DIM
CMP_STRIDE = 16
CMP_BLOCK = 32
CMP_HIDDEN = 128
SEL_BLOCK = 64
TOP_N = 16
WINDOW = 512
Q_BLOCK = 128
D_FF = 4 * D_MODEL
ALPHA = (2 * DEPTH) ** 0.25
LN_EPS = 1e-5
NEG_INF = -1e30
TINY = 1e-30
FORCED_SCORE = 1e4
PROJ_SIZES = (SSM_WIDTH, POOL_WIDTH, NSA_WIDTH) + (KV_WIDTH,) * 6 + (3 * N_HEADS,)
PROJ_WIDTH = sum(PROJ_SIZES)
PROJ_PAD = 3840
GATE_COL = 3584

LANES = 128
SUBLANES = 8
VMEM_LIMIT = 56 * 1024 * 1024


def _cparams(*sem):
    return pltpu.CompilerParams(dimension_semantics=sem, vmem_limit_bytes=VMEM_LIMIT)


def _gelu(x):
    return 0.5 * x * (1.0 + jnp.tanh(math.sqrt(2.0 / math.pi) * (x + 0.044715 * (x * x * x))))


def _sigmoid(x):
    return 1.0 / (1.0 + jnp.exp(-x))


def _layer_norm(z, g, b):
    zc = z - jnp.mean(z, axis=-1, keepdims=True)
    var = jnp.mean(zc * zc, axis=-1, keepdims=True)
    return zc * lax.rsqrt(var + LN_EPS) * g + b


def _proj_kernel(x_ref, w_ref, o_ref, xb_ref):
    @pl.when(pl.program_id(1) == 0)
    def _():
        xb_ref[...] = x_ref[...].astype(BF16)

    o_ref[...] = jnp.dot(xb_ref[...], w_ref[...], preferred_element_type=F32)


def _input_projection(x, w, tm=1024, tn=768):
    m, k = x.shape
    n = w.shape[1]
    return pl.pallas_call(
        _proj_kernel,
        grid=(m // tm, n // tn),
        in_specs=[pl.BlockSpec((tm, k), lambda i, j: (i, 0)),
                  pl.BlockSpec((k, tn), lambda i, j: (0, j))],
        out_specs=pl.BlockSpec((tm, tn), lambda i, j: (i, j)),
        out_shape=jax.ShapeDtypeStruct((m, n), F32),
        scratch_shapes=[pltpu.VMEM((tm, k), BF16)],
        compiler_params=_cparams("parallel", "arbitrary"),
        name="input_projection",
    )(x, w)


def _outproj_ln_kernel(x_ref, ys_ref, yp_ref, yn_ref, w_ref, g_ref, b_ref, o_ref):
    acc = jnp.dot(ys_ref[...], w_ref[0:SSM_WIDTH, :], preferred_element_type=F32)
    acc += jnp.dot(yp_ref[...], w_ref[SSM_WIDTH:SSM_WIDTH + POOL_WIDTH, :], preferred_element_type=F32)
    acc += jnp.dot(yn_ref[...], w_ref[SSM_WIDTH + POOL_WIDTH:, :], preferred_element_type=F32)
    o_ref[...] = _layer_norm(ALPHA * x_ref[...] + acc, g_ref[...], b_ref[...])


def _output_projection_ln(x, y_ssm, y_pool, y_nsa, w, g, b, tm=512):
    m, d = x.shape
    row = lambda i: (i, 0)
    fixed = lambda i: (0, 0)
    return pl.pallas_call(
        _outproj_ln_kernel,
        grid=(m // tm,),
        in_specs=[pl.BlockSpec((tm, d), row),
                  pl.BlockSpec((tm, SSM_WIDTH), row),
                  pl.BlockSpec((tm, POOL_WIDTH), row),
                  pl.BlockSpec((tm, NSA_WIDTH), row),
                  pl.BlockSpec((d, d), fixed),
                  pl.BlockSpec((1, d), fixed),
                  pl.BlockSpec((1, d), fixed)],
        out_specs=pl.BlockSpec((tm, d), row),
        out_shape=jax.ShapeDtypeStruct((m, d), F32),
        compiler_params=_cparams("parallel"),
        name="output_projection_ln",
    )(x, y_ssm, y_pool, y_nsa, w, g, b)


def _mlp_ln_kernel(x_ref, w1_ref, w2_ref, g_ref, b_ref, o_ref, xb_ref, acc_ref):
    f = pl.program_id(1)

    @pl.when(f == 0)
    def _():
        xb_ref[...] = x_ref[...].astype(BF16)
        acc_ref[...] = jnp.zeros_like(acc_ref)

    h = jnp.dot(xb_ref[...], w1_ref[...], preferred_element_type=F32)
    h = jnp.square(jnp.maximum(h, 0.0)).astype(BF16)
    acc_ref[...] += jnp.dot(h, w2_ref[...], preferred_element_type=F32)

    @pl.when(f == pl.num_programs(1) - 1)
    def _():
        o_ref[...] = _layer_norm(ALPHA * x_ref[...] + acc_ref[...], g_ref[...], b_ref[...])


def _mlp_ln(x, w1, w2, g, b, tm=1024, tf=512):
    m, d = x.shape
    ff = w1.shape[1]
    once = pl.Buffered(1)
    return pl.pallas_call(
        _mlp_ln_kernel,
        grid=(m // tm, ff // tf),
        in_specs=[pl.BlockSpec((tm, d), lambda i, f: (i, 0), pipeline_mode=once),
                  pl.BlockSpec((d, tf), lambda i, f: (0, f)),
                  pl.BlockSpec((tf, d), lambda i, f: (f, 0)),
                  pl.BlockSpec((1, d), lambda i, f: (0, 0)),
                  pl.BlockSpec((1, d), lambda i, f: (0, 0))],
        out_specs=pl.BlockSpec((tm, d), lambda i, f: (i, 0), pipeline_mode=once),
        out_shape=jax.ShapeDtypeStruct((m, d), F32),
        scratch_shapes=[pltpu.VMEM((tm, d), BF16), pltpu.VMEM((tm, d), F32)],
        compiler_params=_cparams("parallel", "arbitrary"),
        name="mlp_ln",
    )(x, w1, w2, g, b)


S5_GROUP_BLOCKS = 4
S5_BLOCK_LANES = SSM_FLAT // S5_GROUP_BLOCKS // LANES
S5_SLABS = SSM_FLAT // LANES


def _s5_kernel(u_ref, bre_ref, bim_ref, cre_ref, cim_ref, vec_ref, d_ref, wglu_ref, h0re_ref, h0im_ref,
               y_ref, hre_out, him_out, xre_ref, xim_ref, hre_s, him_s, *, n_seq, steps, carry):
    u = u_ref[...]
    ub = u.astype(BF16)
    cw = SSM_WIDTH // S5_GROUP_BLOCKS
    sw = SSM_FLAT // S5_GROUP_BLOCKS
    for j in range(S5_GROUP_BLOCKS):
        uj = ub[:, j * cw:(j + 1) * cw]
        bur = jnp.dot(uj, bre_ref[j], preferred_element_type=F32)
        bui = jnp.dot(uj, bim_ref[j], preferred_element_type=F32)
        zr = vec_ref[2:3, j * sw:(j + 1) * sw]
        zi = vec_ref[3:4, j * sw:(j + 1) * sw]
        xr = zr * bur - zi * bui
        xi = zr * bui + zi * bur
        for q in range(S5_BLOCK_LANES):
            xre_ref[j * S5_BLOCK_LANES + q] = xr[:, q * LANES:(q + 1) * LANES]
            xim_ref[j * S5_BLOCK_LANES + q] = xi[:, q * LANES:(q + 1) * LANES]

    if carry:
        @pl.when(pl.program_id(0) == 0)
        def _():
            hre_s[...] = h0re_ref[...]
            him_s[...] = h0im_ref[...]
    else:
        hre_s[...] = h0re_ref[...]
        him_s[...] = h0im_ref[...]

    for j in range(S5_GROUP_BLOCKS):
        slabs = [j * S5_BLOCK_LANES + q for q in range(S5_BLOCK_LANES)]
        lanes = [slice(s * LANES, (s + 1) * LANES) for s in slabs]
        ar = [jnp.broadcast_to(vec_ref[0:1, l], (n_seq, LANES)) for l in lanes]
        ai = [jnp.broadcast_to(vec_ref[1:2, l], (n_seq, LANES)) for l in lanes]

        def step(t, h, slabs=slabs, ar=ar, ai=ai):
            rows = pl.ds(t, n_seq, stride=steps) if n_seq > 1 else pl.ds(t, 1)
            out = []
            for q, s in enumerate(slabs):
                hr, hi = h[2 * q], h[2 * q + 1]
                nr = ar[q] * hr - ai[q] * hi + xre_ref[s, rows, :]
                ni = ar[q] * hi + ai[q] * hr + xim_ref[s, rows, :]
                xre_ref[s, rows, :] = nr
                xim_ref[s, rows, :] = ni
                out += [nr, ni]
            return tuple(out)

        h0 = []
        for l in lanes:
            h0 += [hre_s[:, l], him_s[:, l]]
        hT = lax.fori_loop(0, steps, step, tuple(h0), unroll=8)
        for q, l in enumerate(lanes):
            hre_s[:, l] = hT[2 * q]
            him_s[:, l] = hT[2 * q + 1]

    hre_out[...] = hre_s[...]
    him_out[...] = him_s[...]

    ys = []
    for j in range(S5_GROUP_BLOCKS):
        slabs = range(j * S5_BLOCK_LANES, (j + 1) * S5_BLOCK_LANES)
        hr = jnp.concatenate([xre_ref[s] for s in slabs], axis=-1).astype(BF16)
        hi = jnp.concatenate([xim_ref[s] for s in slabs], axis=-1).astype(BF16)
        ys.append(jnp.dot(hr, cre_ref[j], preferred_element_type=F32)
                  - jnp.dot(hi, cim_ref[j], preferred_element_type=F32))
    y = jnp.concatenate(ys, axis=-1) + d_ref[...] * u
    z = _gelu(y)
    gate = _sigmoid(jnp.dot(z.astype(BF16), wglu_ref[...], preferred_element_type=F32))
    y_ref[...] = (z * gate).astype(BF16)


def _s5_mixer(proj, row0, n_rows, n_seq, steps, carry, wts, h0_re, h0_im):
    bre, bim, cre, cim, vec, d, wglu = wts
    chunk = n_seq * steps
    n_chunks = n_rows // chunk
    blk0 = row0 // chunk
    st_rows = h0_re.shape[0]
    st_map = (lambda i: (0, 0)) if carry else (lambda i: (i, 0))
    fixed2 = lambda i: (0, 0)
    fixed3 = lambda i: (0, 0, 0)
    kern = functools.partial(_s5_kernel, n_seq=n_seq, steps=steps, carry=carry)
    return pl.pallas_call(
        kern,
        grid=(n_chunks,),
        in_specs=[pl.BlockSpec((chunk, SSM_WIDTH), lambda i: (blk0 + i, 0)),
                  pl.BlockSpec(bre.shape, fixed3), pl.BlockSpec(bim.shape, fixed3),
                  pl.BlockSpec(cre.shape, fixed3), pl.BlockSpec(cim.shape, fixed3),
                  pl.BlockSpec(vec.shape, fixed2), pl.BlockSpec(d.shape, fixed2),
                  pl.BlockSpec(wglu.shape, fixed2),
                  pl.BlockSpec((n_seq, SSM_FLAT), st_map), pl.BlockSpec((n_seq, SSM_FLAT), st_map)],
        out_specs=[pl.BlockSpec((chunk, SSM_WIDTH), lambda i: (i, 0)),
                   pl.BlockSpec((n_seq, SSM_FLAT), st_map), pl.BlockSpec((n_seq, SSM_FLAT), st_map)],
        out_shape=[jax.ShapeDtypeStruct((n_rows, SSM_WIDTH), BF16),
                   jax.ShapeDtypeStruct((st_rows, SSM_FLAT), F32),
                   jax.ShapeDtypeStruct((st_rows, SSM_FLAT), F32)],
        scratch_shapes=[pltpu.VMEM((S5_SLABS, chunk, LANES), F32), pltpu.VMEM((S5_SLABS, chunk, LANES), F32),
                        pltpu.VMEM((n_seq, SSM_FLAT), F32), pltpu.VMEM((n_seq, SSM_FLAT), F32)],
        compiler_params=_cparams("arbitrary"),
        name="s5_mixer_carry" if carry else "s5_mixer_batch",
    )(proj, bre, bim, cre, cim, vec, d, wglu, h0_re, h0_im)


def _s5_weights(a_re, a_im, log_dt, b_re, b_im, c_re, c_im, d, w_glu):
    dt = jnp.exp(log_dt)[:, None]
    mag = jnp.exp(a_re * dt)
    abar_re, abar_im = mag * jnp.cos(a_im * dt), mag * jnp.sin(a_im * dt)
    den = a_re * a_re + a_im * a_im
    zr = ((abar_re - 1.0) * a_re + abar_im * a_im) / den
    zi = (abar_im * a_re - (abar_re - 1.0) * a_im) / den
    flat = lambda v: v.reshape(1, SSM_FLAT)
    vec = jnp.concatenate([flat(abar_re), flat(abar_im), flat(zr), flat(zi),
                           jnp.zeros((SUBLANES - 4, SSM_FLAT), F32)], axis=0)
    gb = SSM_GROUPS // S5_GROUP_BLOCKS
    eye = jnp.eye(gb, dtype=F32)

    def pack_b(b):
        bb = b.reshape(S5_GROUP_BLOCKS, gb, SSM_STATE, SSM_GROUP_CH)
        m = jnp.einsum('jgph,gk->jghkp', bb, eye)
        return m.reshape(S5_GROUP_BLOCKS, gb * SSM_GROUP_CH, gb * SSM_STATE).astype(BF16)

    def pack_c(c):
        cc = c.reshape(S5_GROUP_BLOCKS, gb, SSM_GROUP_CH, SSM_STATE)
        m = jnp.einsum('jghp,gk->jgpkh', cc, eye)
        return m.reshape(S5_GROUP_BLOCKS, gb * SSM_STATE, gb * SSM_GROUP_CH).astype(BF16)

    return (pack_b(b_re), pack_b(b_im), pack_c(c_re), pack_c(c_im), vec, d.reshape(1, SSM_WIDTH),
            w_glu.astype(BF16))


POOL_HIST = 16


def _pool_seq_kernel(u_ref, buf_ref, w_ref, scale_ref, y_ref, new_ref, xc_ref, *, chunk, pos0):
    i = pl.program_id(0)

    @pl.when(i == 0)
    def _():
        xc_ref[0:POOL_HIST, :] = buf_ref[...]

    u = u_ref[...]
    xc_ref[POOL_HIST:POOL_HIST + chunk, :] = u
    pos = pos0 + i * chunk + lax.broadcasted_iota(jnp.int32, (chunk, POOL_CH), 0)
    outs = []
    for g, wd in enumerate(POOL_WINDOWS):
        lanes = slice(g * POOL_CH, (g + 1) * POOL_CH)
        ug = u[:, lanes]
        acc = ug
        for k in range(1, wd):
            acc = acc + xc_ref[pl.ds(POOL_HIST - k, chunk), lanes]
        mix = acc / jnp.minimum(wd, pos + 1).astype(F32) - ug
        outs.append(jnp.dot(mix.astype(BF16), w_ref[g], preferred_element_type=F32))
    y_ref[...] = (jnp.concatenate(outs, axis=-1) * scale_ref[...]).astype(BF16)
    tail = xc_ref[chunk:chunk + POOL_HIST, :]
    xc_ref[0:POOL_HIST, :] = tail
    new_ref[...] = tail


def _pool_mixer_seq(proj, row0, n_rows, pos0, buf16, w, scale, chunk=256):
    blk0 = row0 // chunk
    kern = functools.partial(_pool_seq_kernel, chunk=chunk, pos0=pos0)
    return pl.pallas_call(
        kern,
        grid=(n_rows // chunk,),
        in_specs=[pl.BlockSpec((chunk, POOL_WIDTH), lambda i: (blk0 + i, 1)),
                  pl.BlockSpec((POOL_HIST, POOL_WIDTH), lambda i: (0, 0)),
                  pl.BlockSpec(w.shape, lambda i: (0, 0, 0)),
                  pl.BlockSpec((1, POOL_WIDTH), lambda i: (0, 0))],
        out_specs=[pl.BlockSpec((chunk, POOL_WIDTH), lambda i: (i, 0)),
                   pl.BlockSpec((POOL_HIST, POOL_WIDTH), lambda i: (0, 0))],
        out_shape=[jax.ShapeDtypeStruct((n_rows, POOL_WIDTH), BF16),
                   jax.ShapeDtypeStruct((POOL_HIST, POOL_WIDTH), F32)],
        scratch_shapes=[pltpu.VMEM((POOL_HIST + chunk, POOL_WIDTH), F32)],
        compiler_params=_cparams("arbitrary"),
        name="pool_mixer_seq",
    )(proj, buf16, w, scale)


def _pool_batch_kernel(u_ref, buf_ref, w_ref, scale_ref, y_ref, new_ref, *, steps, pos0):
    def xrow(j, lanes):
        return buf_ref[j, :, lanes] if j < POOL_BUF else u_ref[j - POOL_BUF, :, lanes]

    nb = u_ref.shape[1]
    for g, wd in enumerate(POOL_WINDOWS):
        lanes = slice(g * POOL_CH, (g + 1) * POOL_CH)
        mixes = []
        for t in range(steps):
            ug = u_ref[t, :, lanes]
            acc = ug
            for k in range(1, wd):
                acc = acc + xrow(POOL_BUF + t - k, lanes)
            mixes.append(acc / float(min(wd, pos0 + t + 1)) - ug)
        mix = jnp.concatenate(mixes, axis=0).astype(BF16)
        yg = jnp.dot(mix, w_ref[g], preferred_element_type=F32) * scale_ref[:, lanes]
        for t in range(steps):
            y_ref[t, :, lanes] = yg[t * nb:(t + 1) * nb].astype(BF16)
    for j in range(POOL_BUF):
        new_ref[j] = xrow(steps + j, slice(None))


def _pool_mixer_batch(u_t, buf_t, pos0, w, scale):
    steps, nb, _ = u_t.shape
    kern = functools.partial(_pool_batch_kernel, steps=steps, pos0=pos0)
    full3 = lambda i: (0, 0, 0)
    return pl.pallas_call(
        kern,
        grid=(1,),
        in_specs=[pl.BlockSpec(u_t.shape, full3), pl.BlockSpec(buf_t.shape, full3),
                  pl.BlockSpec(w.shape, full3), pl.BlockSpec((1, POOL_WIDTH), lambda i: (0, 0))],
        out_specs=[pl.BlockSpec(u_t.shape, full3), pl.BlockSpec(buf_t.shape, full3)],
        out_shape=[jax.ShapeDtypeStruct(u_t.shape, BF16), jax.ShapeDtypeStruct(buf_t.shape, F32)],
        compiler_params=_cparams("arbitrary"),
        name="pool_mixer_batch",
    )(u_t, buf_t, w, scale)


CHUNK_FLAT = CMP_STRIDE * KV_WIDTH
CMP_HID_ALL = N_KV * CMP_HIDDEN


def _compress_kernel(*refs, n_pages):
    if n_pages > 1:
        refs = refs[1:]
    page_refs = refs[:n_pages]
    wbig_ref, pe_ref, w1_ref, w2_ref, out_ref, b_ref = refs[n_pages:]
    chunks = jnp.concatenate([r[...] for r in page_refs], axis=0) if n_pages > 1 else page_refs[0][...]
    n = chunks.shape[0]
    ab = jnp.dot(chunks.astype(BF16), wbig_ref[...], preferred_element_type=F32)
    b_ref[0:n, :] = ab[:, CMP_HID_ALL:]
    b_ref[n:n + SUBLANES, :] = jnp.zeros((SUBLANES, CMP_HID_ALL), F32)
    hid0 = jnp.dot(pe_ref[...], w1_ref[...], preferred_element_type=F32)[0:1]
    hid = ab[:, :CMP_HID_ALL] + b_ref[pl.ds(1, n), :] + jnp.concatenate([hid0] * N_KV, axis=-1)
    out = jnp.dot(_gelu(hid).astype(BF16), w2_ref[...], preferred_element_type=F32)
    row = lax.broadcasted_iota(jnp.int32, out.shape, 0)
    out_ref[...] = jnp.where(row < n - 1, out, 0.0).astype(out_ref.dtype)


def _compress_weights(pe, w1, w2):
    w1b = w1.reshape(2, CMP_STRIDE, HEAD_DIM, CMP_HIDDEN)
    eye = jnp.eye(N_KV, dtype=F32)
    big = jnp.einsum('msdh,kj->mskdjh', w1b, eye).reshape(2, CHUNK_FLAT, CMP_HID_ALL)
    wbig = jnp.concatenate([big[0], big[1]], axis=-1).astype(BF16)
    w2big = jnp.einsum('hd,kj->khjd', w2, eye).reshape(CMP_HID_ALL, KV_WIDTH).astype(BF16)
    pe8 = jnp.concatenate([pe.reshape(1, CMP_BLOCK * HEAD_DIM),
                           jnp.zeros((SUBLANES - 1, CMP_BLOCK * HEAD_DIM), F32)], axis=0).astype(BF16)
    return wbig, pe8, w1.astype(BF16), w2big


def _compress_seq(chunks, wts):
    n = chunks.shape[0]
    wbig, pe8, w1, w2big = wts
    full = lambda i: (0, 0)
    return pl.pallas_call(
        functools.partial(_compress_kernel, n_pages=1),
        grid=(1,),
        in_specs=[pl.BlockSpec(chunks.shape, full), pl.BlockSpec(wbig.shape, full), pl.BlockSpec(pe8.shape, full),
                  pl.BlockSpec(w1.shape, full), pl.BlockSpec(w2big.shape, full)],
        out_specs=pl.BlockSpec((n, KV_WIDTH), full),
        out_shape=jax.ShapeDtypeStruct((n, KV_WIDTH), BF16),
        scratch_shapes=[pltpu.VMEM((n + SUBLANES, CMP_HID_ALL), F32)],
        compiler_params=_cparams("arbitrary"),
        name="compress_seq",
    )(chunks, wbig, pe8, w1, w2big)


def _compress_paged(cache, layer, page_table, wts):
    nb, n_pages = page_table.shape
    cpp = PAGE_SIZE // CMP_STRIDE
    n = n_pages * cpp
    wbig, pe8, w1, w2big = wts
    full = lambda b, pt: (0, 0)

    def page_spec(p):
        return pl.BlockSpec((None, None, cpp, CHUNK_FLAT), lambda b, pt: (layer, pt[b, p], 0, 0))

    grid_spec = pltpu.PrefetchScalarGridSpec(
        num_scalar_prefetch=1,
        grid=(nb,),
        in_specs=[page_spec(p) for p in range(n_pages)]
        + [pl.BlockSpec(wbig.shape, full), pl.BlockSpec(pe8.shape, full), pl.BlockSpec(w1.shape, full),
           pl.BlockSpec(w2big.shape, full)],
        out_specs=pl.BlockSpec((None, n, KV_WIDTH), lambda b, pt: (b, 0, 0)),
        scratch_shapes=[pltpu.VMEM((n + SUBLANES, CMP_HID_ALL), F32)],
    )
    return pl.pallas_call(
        functools.partial(_compress_kernel, n_pages=n_pages),
        grid_spec=grid_spec,
        out_shape=jax.ShapeDtypeStruct((nb, n, KV_WIDTH), BF16),
        compiler_params=_cparams("arbitrary"),
        name="compress_paged",
    )(page_table, *([cache] * n_pages), wbig, pe8, w1, w2big)


def _alibi_slope(h):
    return 2.0 ** (-8.0 * (h + 1) / N_HEADS)


NT_DIMS = (((1,), (1,)), ((), ()))
QROWS = GQA * Q_BLOCK
KEY_TILE = 128
WIN_TILES = WINDOW // KEY_TILE
KEY_STEP = 2 * KEY_TILE
T_FULL, T_DIAG, T_OLD, T_NONE = 0, 1, 2, 3
LOG2E = math.log2(math.e)
V_ROWS = HEAD_DIM + 16


def _nsa_seq_kernel(q_ref, gate_ref, kc_ref, vct_ref, ks_ref, vst_ref, kw_ref, vwt_ref, o_ref,
                    qx_ref, bias_ref, sbuf_ref, m_ref, acc_ref, oc_ref, os_ref, ow_ref, *, n_cmp_pad, n_sel, n_steps):
    i = pl.program_id(0)
    base = i * Q_BLOCK

    lane = lax.broadcasted_iota(jnp.int32, (1, QROWS), 1)
    qq_i = lane % Q_BLOCK
    qq = qq_i.astype(F32)
    g_lane = lane // Q_BLOCK

    def slope_row(k):
        r = jnp.full((1, QROWS), LOG2E * _alibi_slope(GQA * k + GQA - 1), F32)
        for g in range(GQA - 1):
            r = jnp.where(g_lane == g, LOG2E * _alibi_slope(GQA * k + g), r)
        return r

    slopes = [slope_row(k) for k in range(N_KV)]

    @pl.when(i == 0)
    def _():
        kk = lax.broadcasted_iota(jnp.int32, (KEY_TILE, QROWS), 0).astype(F32)
        for half in range(2):
            for k in range(N_KV):
                b = -slopes[k] * (qq - (kk + float(half * KEY_TILE)))
                bias_ref[half, T_FULL, k] = b
                bias_ref[half, T_DIAG, k] = jnp.where(kk <= qq, b, NEG_INF)
                bias_ref[half, T_OLD, k] = jnp.where(kk > qq, b, NEG_INF)
                bias_ref[half, T_NONE, k] = jnp.full((KEY_TILE, QROWS), NEG_INF, F32)

    qs = q_ref[...] * (LOG2E * HEAD_DIM ** -0.5)
    low_half = lax.broadcasted_iota(jnp.int32, (Q_BLOCK, LANES), 1) < HEAD_DIM
    for k in range(N_KV):
        keep = low_half if k % 2 == 0 else jnp.logical_not(low_half)
        parts = []
        for g in range(GQA):
            h = GQA * k + g
            pair = qs[:, (h // 2) * LANES:(h // 2 + 1) * LANES]
            if h % 2 != k % 2:
                pair = pltpu.roll(pair, HEAD_DIM, 1)
            parts.append(jnp.where(keep, pair, 0.0).astype(BF16))
        qx_ref[k, :, 0:LANES] = jnp.concatenate(parts, axis=0)

    n_io = lax.broadcasted_iota(jnp.int32, (n_cmp_pad, QROWS), 0)
    distc = (base.astype(F32) + qq) - (n_io.astype(F32) * CMP_STRIDE + (CMP_BLOCK - 1) / 2.0)
    visc = (n_io * CMP_STRIDE + (CMP_BLOCK - 1)) <= (base + qq_i)
    sj_o = lax.broadcasted_iota(jnp.int32, (n_sel, n_cmp_pad), 0)
    nn_o = lax.broadcasted_iota(jnp.int32, (n_sel, n_cmp_pad), 1)
    ovt = jnp.where((nn_o * CMP_STRIDE < (sj_o + 1) * SEL_BLOCK)
                    & (nn_o * CMP_STRIDE + (CMP_BLOCK - 1) >= sj_o * SEL_BLOCK), 1.0, 0.0).astype(BF16)
    sjf = lax.broadcasted_iota(jnp.int32, (n_sel, Q_BLOCK), 0).astype(F32)
    curf = ((base + lax.broadcasted_iota(jnp.int32, (1, Q_BLOCK), 1)) // SEL_BLOCK).astype(F32)
    visible = sjf <= curf
    forced = visible & ((sjf == 0.0) | (sjf == curf) | (sjf == curf - 1.0))

    for k in range(N_KV):
        kl = slice((k // 2) * LANES, (k // 2 + 1) * LANES)
        vr = slice(k * HEAD_DIM, (k + 1) * HEAD_DIM)
        s = lax.dot_general(kc_ref[:, kl], qx_ref[k, :, 0:LANES], NT_DIMS, preferred_element_type=F32)
        s = jnp.where(visc, s - slopes[k] * distc, NEG_INF)
        m = jnp.maximum(jnp.max(s, axis=0, keepdims=True), 0.5 * NEG_INF)
        p = jnp.exp2(s - m)
        p = p * (1.0 / jnp.maximum(jnp.sum(p, axis=0, keepdims=True), TINY))
        oc_ref[k] = jnp.dot(vct_ref[vr, :], p.astype(BF16), preferred_element_type=F32)
        psum = p[:, 0:Q_BLOCK]
        for g in range(1, GQA):
            psum = psum + p[:, g * Q_BLOCK:(g + 1) * Q_BLOCK]
        p_hi = psum.astype(BF16)
        p_lo = (psum - p_hi.astype(F32)).astype(BF16)
        imp = (jnp.dot(ovt, p_hi, preferred_element_type=F32)
               + jnp.dot(ovt, p_lo, preferred_element_type=F32))
        score = jnp.where(forced, FORCED_SCORE, jnp.where(visible, imp, -1.0))
        sel = jnp.zeros_like(score)
        for _ in range(min(TOP_N, n_sel)):
            mx = jnp.max(score, axis=0, keepdims=True)
            idx = jnp.min(jnp.where(score == mx, sjf, 1e9), axis=0, keepdims=True)
            hit = sjf == idx
            sel = jnp.where(hit & (mx >= 0.0), 1.0, sel)
            score = jnp.where(hit, -2.0, score)
        selq = jnp.where(sel > 0.0, 0.0, NEG_INF).T.astype(BF16)
        if n_sel < LANES:
            selq = jnp.concatenate([selq, jnp.zeros((Q_BLOCK, LANES - n_sel), BF16)], axis=1)
        qx_ref[k, :, LANES:2 * LANES] = jnp.concatenate([selq] * GQA, axis=0)

    blk_of_key = lax.broadcasted_iota(jnp.int32, (KEY_STEP, LANES), 0) // SEL_BLOCK
    lane_id = lax.broadcasted_iota(jnp.int32, (KEY_STEP, LANES), 1)

    def flash(k_ref, vt_ref, j_lo, use_sel, out_ref):
        m_ref[...] = jnp.full(m_ref.shape, NEG_INF, F32)
        acc_ref[...] = jnp.zeros(acc_ref.shape, F32)

        def tile_type(t):
            ty = jnp.where(t == i, T_DIAG, jnp.where(t > i, T_NONE, T_FULL))
            if not use_sel:
                ty = jnp.where(t == i - WIN_TILES, T_OLD, jnp.where(t < i - WIN_TILES, T_NONE, ty))
            return ty

        def scores(j, heads):
            jj = jnp.minimum(j, n_steps - 1)
            ty0, ty1 = tile_type(2 * jj), tile_type(2 * jj + 1)
            if use_sel:
                onehot = jnp.where(lane_id == blk_of_key + (KEY_STEP // SEL_BLOCK) * jj, 1.0, 0.0).astype(BF16)
            out = []
            for k in heads:
                kl = slice((k // 2) * LANES, (k // 2 + 1) * LANES)
                if use_sel:
                    kx = jnp.concatenate([k_ref[jj, :, kl], onehot], axis=1)
                    s = lax.dot_general(kx, qx_ref[k], NT_DIMS, preferred_element_type=F32)
                else:
                    s = lax.dot_general(k_ref[jj, :, kl], qx_ref[k, :, 0:LANES], NT_DIMS,
                                        preferred_element_type=F32)
                out.append(s + jnp.concatenate([bias_ref[0, ty0, k], bias_ref[1, ty1, k]], axis=0))
            return out

        def softmax_pv(j, k, s):
            c = slopes[k] * (i * KEY_TILE - j * KEY_STEP).astype(F32)
            m_old = m_ref[k] + c
            m_new = jnp.maximum(m_old, jnp.max(s, axis=0, keepdims=True))
            alpha = jnp.exp2(m_old - m_new)
            p = jnp.exp2(s - m_new).astype(BF16)
            m_ref[k] = m_new - c
            acc_ref[k] = alpha * acc_ref[k] + jnp.dot(vt_ref[j, k * V_ROWS:(k + 1) * V_ROWS, :], p,
                                                      preferred_element_type=F32)

        sbuf_ref[0], sbuf_ref[1] = scores(j_lo, (0, 1))

        def body(j, carry):
            (s2,) = scores(j, (2,))
            softmax_pv(j, 0, sbuf_ref[0])
            (s3,) = scores(j, (3,))
            softmax_pv(j, 1, sbuf_ref[1])
            n0, n1 = scores(j + 1, (0, 1))
            sbuf_ref[0] = n0
            softmax_pv(j, 2, s2)
            sbuf_ref[1] = n1
            softmax_pv(j, 3, s3)
            return carry

        lax.fori_loop(j_lo, i // 2 + 1, body, 0)
        for k in range(N_KV):
            out_ref[k] = acc_ref[k, 0:HEAD_DIM] * (1.0 / jnp.maximum(acc_ref[k, HEAD_DIM:HEAD_DIM + 1], TINY))

    flash(ks_ref, vst_ref, 0, True, os_ref)
    flash(kw_ref, vwt_ref, jnp.maximum(i - WIN_TILES, 0) // 2, False, ow_ref)

    gate = _sigmoid(gate_ref[...])
    blocks = []
    for k in range(N_KV):
        for g in range(GQA):
            h = GQA * k + g
            sl = slice(g * Q_BLOCK, (g + 1) * Q_BLOCK)
            blocks.append(oc_ref[k, :, sl] * gate[3 * h:3 * h + 1]
                          + os_ref[k, :, sl] * gate[3 * h + 1:3 * h + 2]
                          + ow_ref[k, :, sl] * gate[3 * h + 2:3 * h + 3])
    o_ref[...] = jnp.concatenate(blocks, axis=0).T.astype(BF16)


def _value_steps(v):
    steps = v.shape[0] // KEY_STEP
    vt = v.astype(BF16).reshape(steps, KEY_STEP, N_KV, HEAD_DIM).transpose(0, 2, 3, 1)
    ones = jnp.ones((steps, N_KV, 1, KEY_STEP), BF16)
    pad = jnp.zeros((steps, N_KV, V_ROWS - HEAD_DIM - 1, KEY_STEP), BF16)
    return jnp.concatenate([vt, ones, pad], axis=2).reshape(steps, N_KV * V_ROWS, KEY_STEP)


def _nsa_seq(proj, gate_t, kc, vct, ks_t, vst_t, kw_t, vwt_t, n_rows):
    n_tiles = n_rows // Q_BLOCK
    assert n_tiles % 2 == 0
    n_cmp_pad = kc.shape[0]
    n_sel = n_rows // SEL_BLOCK
    assert n_sel <= LANES
    kern = functools.partial(_nsa_seq_kernel, n_cmp_pad=n_cmp_pad, n_sel=n_sel, n_steps=n_tiles // 2)
    c2 = lambda i: (0, 0)
    c3 = lambda i: (0, 0, 0)
    st = (N_KV, HEAD_DIM, QROWS)
    return pl.pallas_call(
        kern,
        grid=(n_tiles,),
        in_specs=[pl.BlockSpec((Q_BLOCK, NSA_WIDTH), lambda i: (i, 1)),
                  pl.BlockSpec((3 * N_HEADS, Q_BLOCK), lambda i: (0, i)),
                  pl.BlockSpec(kc.shape, c2), pl.BlockSpec(vct.shape, c2),
                  pl.BlockSpec(ks_t.shape, c3), pl.BlockSpec(vst_t.shape, c3),
                  pl.BlockSpec(kw_t.shape, c3), pl.BlockSpec(vwt_t.shape, c3)],
        out_specs=pl.BlockSpec((Q_BLOCK, NSA_WIDTH), lambda i: (i, 0)),
        out_shape=jax.ShapeDtypeStruct((n_rows, NSA_WIDTH), BF16),
        scratch_shapes=[pltpu.VMEM((N_KV, QROWS, 2 * LANES), BF16),
                        pltpu.VMEM((2, 4, N_KV, KEY_TILE, QROWS), F32),
                        pltpu.VMEM((2, KEY_STEP, QROWS), F32),
                        pltpu.VMEM((N_KV, 1, QROWS), F32), pltpu.VMEM((N_KV, V_ROWS, QROWS), F32),
                        pltpu.VMEM(st, F32), pltpu.VMEM(st, F32), pltpu.VMEM(st, F32)],
        compiler_params=_cparams("arbitrary"),
        name="nsa_seq",
    )(proj, gate_t, kc, vct, ks_t, vst_t, kw_t, vwt_t)


def _nsa_batch_kernel(pt_ref, *refs, n_pages, steps, pos0, group):
    del pt_ref
    (kwin_ref, vwin_ref, kc_ref, vc_ref, q_ref, ksn_ref, vsn_ref, kwn_ref, vwn_ref, gate_ref,
     o_ref, kwo_ref, vwo_ref, bsel_ref, bwin_ref, bcmp_ref, e_ref, ov_ref) = refs[2 * group * n_pages:]
    rows = N_HEADS * steps
    past = n_pages * PAGE_SIZE
    nk = past + KEY_TILE
    nw = WINDOW + KEY_TILE

    @pl.when(pl.program_id(0) == 0)
    def _():
        def tables(width):
            r = lax.broadcasted_iota(jnp.int32, (rows, width), 0)
            c = lax.broadcasted_iota(jnp.int32, (rows, width), 1)
            slope = jnp.exp((-8.0 * math.log(2.0) / N_HEADS) * (r // steps + 1).astype(F32))
            return slope, pos0 + r % steps, c

        slope, qpos, key = tables(nk)
        bsel_ref[...] = jnp.where(key <= qpos, -slope * (qpos - key).astype(F32), NEG_INF)
        slope, qpos, w = tables(nw)
        kpos = pos0 - WINDOW + w
        d = qpos - kpos
        bwin_ref[...] = jnp.where((d >= 0) & (d < WINDOW) & (kpos >= 0), -slope * d.astype(F32), NEG_INF)
        slope, qpos, n = tables(LANES)
        c_mid = n.astype(F32) * CMP_STRIDE + (CMP_BLOCK - 1) / 2.0
        bcmp_ref[...] = jnp.where(n * CMP_STRIDE + (CMP_BLOCK - 1) <= qpos,
                                  -slope * (qpos.astype(F32) - c_mid), NEG_INF)
        sj = lax.broadcasted_iota(jnp.int32, (LANES, nk), 0)
        key = lax.broadcasted_iota(jnp.int32, (LANES, nk), 1)
        e_ref[...] = jnp.where(key // SEL_BLOCK == sj, 1.0, 0.0).astype(BF16)
        n = lax.broadcasted_iota(jnp.int32, (LANES, LANES), 0)
        sj = lax.broadcasted_iota(jnp.int32, (LANES, LANES), 1)
        ov_ref[...] = jnp.where((n * CMP_STRIDE < (sj + 1) * SEL_BLOCK)
                                & (n * CMP_STRIDE + (CMP_BLOCK - 1) >= sj * SEL_BLOCK), 1.0, 0.0).astype(BF16)

    def _nsa_batch_one(bl):
        ks_pages = refs[bl * n_pages:(bl + 1) * n_pages]
        vs_pages = refs[(group + bl) * n_pages:(group + bl + 1) * n_pages]
        tok = slice(bl * steps, (bl + 1) * steps)
        qs = q_ref[tok, :] * (HEAD_DIM ** -0.5)
        low_half = lax.broadcasted_iota(jnp.int32, (steps, LANES), 1) < HEAD_DIM
        zero = jnp.zeros((steps, LANES), F32)
        pieces = []
        for k in range(N_KV):
            keep = low_half if k % 2 == 0 else jnp.logical_not(low_half)
            for g in range(GQA):
                h = GQA * k + g
                pair = qs[:, (h // 2) * LANES:(h // 2 + 1) * LANES]
                if h % 2 != k % 2:
                    pair = pltpu.roll(pair, HEAD_DIM, 1)
                blk = jnp.where(keep, pair, 0.0)
                pieces.append(jnp.concatenate([blk, zero] if k // 2 == 0 else [zero, blk], axis=1))
        qb = jnp.concatenate(pieces, axis=0).astype(BF16)

        def new_tile(ref):
            return jnp.concatenate([ref[tok, :], jnp.zeros((KEY_TILE - steps, KV_WIDTH), F32)], axis=0).astype(BF16)

        def softmax_pv(s, v_parts, transposed):
            m = jnp.max(s, axis=-1, keepdims=True)
            p = jnp.exp(s - m)
            l = jnp.sum(p, axis=-1, keepdims=True)
            acc = None
            off = 0
            for v, v_t in zip(v_parts, transposed):
                n = v.shape[1] if v_t else v.shape[0]
                pb = p[:, off:off + n].astype(BF16)
                part = (lax.dot_general(pb, v, NT_DIMS, preferred_element_type=F32) if v_t
                        else jnp.dot(pb, v, preferred_element_type=F32))
                acc = part if acc is None else acc + part
                off += n
            return acc * (1.0 / jnp.maximum(l, TINY))

        bc = bcmp_ref[...]
        s_c = lax.dot_general(qb, kc_ref[bl], NT_DIMS, preferred_element_type=F32) + bc
        kt_all = jnp.concatenate([r[...] for r in ks_pages], axis=1).astype(BF16)
        s_s = jnp.concatenate([jnp.dot(qb, kt_all, preferred_element_type=F32),
                               lax.dot_general(qb, new_tile(ksn_ref), NT_DIMS, preferred_element_type=F32)],
                              axis=1) + bsel_ref[...]
        s_w = jnp.concatenate([jnp.dot(qb, kwin_ref[bl].astype(BF16), preferred_element_type=F32),
                               lax.dot_general(qb, new_tile(kwn_ref), NT_DIMS, preferred_element_type=F32)],
                              axis=1) + bwin_ref[...]
        yield

        m = jnp.max(s_c, axis=-1, keepdims=True)
        p = jnp.exp(s_c - m) * jnp.where(bc > 0.5 * NEG_INF, 1.0, 0.0)
        p = p * (1.0 / jnp.maximum(jnp.sum(p, axis=-1, keepdims=True), TINY))
        o_c = jnp.dot(p.astype(BF16), vc_ref[bl], preferred_element_type=F32)
        grp = GQA * steps
        psum = []
        for k in range(N_KV):
            acc = p[k * grp:k * grp + steps]
            for g in range(1, GQA):
                acc = acc + p[k * grp + g * steps:k * grp + (g + 1) * steps]
            psum.append(acc)
        psum = jnp.concatenate(psum, axis=0)
        p_hi = psum.astype(BF16)
        p_lo = (psum - p_hi.astype(F32)).astype(BF16)
        imp = (jnp.dot(p_hi, ov_ref[...], preferred_element_type=F32)
               + jnp.dot(p_lo, ov_ref[...], preferred_element_type=F32))
        yield

        o_w = softmax_pv(s_w, [vwin_ref[bl].astype(BF16), new_tile(vwn_ref)], [True, False])
        lane_w = lax.broadcasted_iota(jnp.int32, (KV_WIDTH, WINDOW), 1)
        for src, nw_ref, dst in ((kwin_ref, kwn_ref, kwo_ref), (vwin_ref, vwn_ref, vwo_ref)):
            new_t = jnp.concatenate([nw_ref[tok, :], jnp.zeros((LANES - steps, KV_WIDTH), F32)], axis=0).T
            new_t = jnp.concatenate([new_t] * (WINDOW // LANES), axis=1)
            shifted = pltpu.roll(src[bl], WINDOW - steps, 1)
            dst[bl] = jnp.where(lane_w < WINDOW - steps, shifted, pltpu.roll(new_t, WINDOW - steps, 1))
        yield

        sjf = lax.broadcasted_iota(jnp.int32, imp.shape, 1).astype(F32)
        step_of_row = lax.broadcasted_iota(jnp.int32, imp.shape, 0) % steps
        curf = ((pos0 + step_of_row) // SEL_BLOCK).astype(F32)
        visible = sjf <= curf
        forced = visible & ((sjf == 0.0) | (sjf == curf) | (sjf == curf - 1.0))
        score = jnp.where(forced, FORCED_SCORE, jnp.where(visible, imp, -1.0))
        n_blocks = -(-(past + steps) // SEL_BLOCK)
        rank = jnp.zeros_like(score)
        for jp in range(n_blocks):
            other = jnp.broadcast_to(score[:, jp:jp + 1], score.shape)
            beats = (other > score) | ((other == score) & (sjf > float(jp)))
            rank = rank + jnp.where(beats, 1.0, 0.0)
        sel = jnp.where((rank < float(TOP_N)) & (score >= 0.0), 1.0, 0.0)
        selk = jnp.dot(sel.astype(BF16), e_ref[...], preferred_element_type=F32)
        yield

        selb = jnp.where(selk > 0.5, 0.0, NEG_INF)
        sb = jnp.concatenate([selb[k * steps:(k + 1) * steps] for k in range(N_KV) for _ in range(GQA)], axis=0)
        vt_all = jnp.concatenate([r[...] for r in vs_pages], axis=1).astype(BF16)
        o_s = softmax_pv(s_s + sb, [vt_all, new_tile(vsn_ref)], [True, False])
        yield

        gates = _sigmoid(gate_ref[bl])
        y = o_c * gates[:, 0:1] + o_s * gates[:, 1:2] + o_w * gates[:, 2:3]
        for h in range(N_HEADS):
            k = h // GQA
            o_ref[tok, h * HEAD_DIM:(h + 1) * HEAD_DIM] = y[h * steps:(h + 1) * steps, k * HEAD_DIM:(k + 1) * HEAD_DIM]

    pending = [_nsa_batch_one(bl) for bl in range(group)]
    while pending:
        pending = [g for g in pending if next(g, True) is None]


def _nsa_batch(proj, row0, steps, pos0, gate_rows, kc, vc, ks_cache, vs_cache, layer, page_table, kwin, vwin,
               group=2):
    nb, n_pages = page_table.shape
    rows = N_HEADS * steps
    nk = n_pages * PAGE_SIZE + KEY_TILE
    tok = group * steps
    blk0 = row0 // tok
    kern = functools.partial(_nsa_batch_kernel, n_pages=n_pages, steps=steps, pos0=pos0, group=group)

    def page_spec(bl, p):
        return pl.BlockSpec((None, None, KV_WIDTH, PAGE_SIZE), lambda b, pt: (layer, pt[group * b + bl, p], 0, 0))

    pages = [page_spec(bl, p) for bl in range(group) for p in range(n_pages)]

    per_b = lambda b, pt: (b, 0, 0)
    kv_col0 = sum(PROJ_SIZES[:3]) // KV_WIDTH

    def new_spec(j):
        return pl.BlockSpec((tok, KV_WIDTH), lambda b, pt: (blk0 + b, kv_col0 + j))

    grid_spec = pltpu.PrefetchScalarGridSpec(
        num_scalar_prefetch=1,
        grid=(nb // group,),
        in_specs=pages + pages
        + [pl.BlockSpec((group, KV_WIDTH, WINDOW), per_b), pl.BlockSpec((group, KV_WIDTH, WINDOW), per_b),
           pl.BlockSpec((group,) + kc.shape[1:], per_b), pl.BlockSpec((group,) + vc.shape[1:], per_b),
           pl.BlockSpec((tok, NSA_WIDTH), lambda b, pt: (blk0 + b, 1)),
           new_spec(2), new_spec(3), new_spec(4), new_spec(5),
           pl.BlockSpec((group, rows, SUBLANES), per_b)],
        out_specs=[pl.BlockSpec((tok, NSA_WIDTH), lambda b, pt: (b, 0)),
                   pl.BlockSpec((group, KV_WIDTH, WINDOW), per_b), pl.BlockSpec((group, KV_WIDTH, WINDOW), per_b)],
        scratch_shapes=[pltpu.VMEM((rows, nk), F32), pltpu.VMEM((rows, WINDOW + KEY_TILE), F32),
                        pltpu.VMEM((rows, LANES), F32), pltpu.VMEM((LANES, nk), BF16),
                        pltpu.VMEM((LANES, LANES), BF16)],
    )
    return pl.pallas_call(
        kern,
        grid_spec=grid_spec,
        out_shape=[jax.ShapeDtypeStruct((nb * steps, NSA_WIDTH), F32),
                   jax.ShapeDtypeStruct((nb, KV_WIDTH, WINDOW), F32),
                   jax.ShapeDtypeStruct((nb, KV_WIDTH, WINDOW), F32)],
        compiler_params=_cparams("arbitrary"),
        name="nsa_batch",
    )(page_table, *([ks_cache] * (group * n_pages)), *([vs_cache] * (group * n_pages)), kwin, vwin, kc, vc,
      proj, proj, proj, proj, proj, gate_rows)


S5_SEQ_CHUNK = 256
S5_BATCH_SEQS = 16


def kernel(x_prompt, x_sample, cache_k_cmp, cache_v_cmp, cache_k_sel, cache_v_sel, cache_k_win, cache_v_win, state_ssm_re, state_ssm_im, state_pool, page_table, w_in, ssm_a_re, ssm_a_im, ssm_log_dt, ssm_b_re, ssm_b_im, ssm_c_re, ssm_c_im, ssm_d, ssm_w_glu, pool_w, pool_scale, cmp_pe, cmp_w1, cmp_w2, w_out, ln1_g, ln1_b, mlp_w1, mlp_w2, ln2_g, ln2_b):
    bp, lp, d = x_prompt.shape
    nb, steps, _ = x_sample.shape
    assert bp == 1 and d == D_MODEL and steps < CMP_STRIDE
    n_p, n_s = bp * lp, nb * steps
    n_phys = cache_k_cmp.shape[1]
    past_len = page_table.shape[1] * PAGE_SIZE
    assert cache_k_win.shape[2] == WINDOW and past_len >= WINDOW

    x = jnp.concatenate([x_prompt.reshape(n_p, d), x_sample.reshape(n_s, d)], axis=0)
    cpp = PAGE_SIZE // CMP_STRIDE
    kcmp_pages = cache_k_cmp.reshape(DEPTH, n_phys, cpp, CHUNK_FLAT)
    vcmp_pages = cache_v_cmp.reshape(DEPTH, n_phys, cpp, CHUNK_FLAT)
    rows_last = lambda c: c.transpose(0, 1, 3, 4, 2).reshape(c.shape[0], c.shape[1], KV_WIDTH, c.shape[2])
    ksel_pages, vsel_pages = rows_last(cache_k_sel), rows_last(cache_v_sel)
    kwin_t, vwin_t = rows_last(cache_k_win), rows_last(cache_v_win)
    kv_col0 = sum(PROJ_SIZES[:3])
    zero_state = jnp.zeros((1, SSM_FLAT), F32)
    zero_buf = jnp.zeros((POOL_HIST, POOL_WIDTH), F32)
    row = lambda v: v.reshape(1, -1)

    new_p, new_s = [], []
    for l in range(DEPTH):
        w_in_l = jnp.pad(w_in[l], ((0, 0), (0, PROJ_PAD - PROJ_WIDTH))).astype(BF16)
        proj = _input_projection(x, w_in_l)
        kv = [proj[:, kv_col0 + j * KV_WIDTH:kv_col0 + (j + 1) * KV_WIDTH] for j in range(6)]
        gate_logits = proj[:, GATE_COL:GATE_COL + 3 * N_HEADS]

        s5w = _s5_weights(ssm_a_re[l], ssm_a_im[l], ssm_log_dt[l], ssm_b_re[l], ssm_b_im[l], ssm_c_re[l],
                          ssm_c_im[l], ssm_d[l], ssm_w_glu[l])
        ys_p, hr_p, hi_p = _s5_mixer(proj, 0, n_p, 1, S5_SEQ_CHUNK, True, s5w, zero_state, zero_state)
        ys_s, hr_s, hi_s = _s5_mixer(proj, n_p, n_s, S5_BATCH_SEQS, steps, False, s5w,
                                     state_ssm_re[l].reshape(nb, SSM_FLAT), state_ssm_im[l].reshape(nb, SSM_FLAT))

        pw, psc = pool_w[l].astype(BF16), row(pool_scale[l])
        yp_p, pool_p = _pool_mixer_seq(proj, 0, n_p, 0, zero_buf, pw, psc)
        u_t = proj[n_p:, SSM_WIDTH:SSM_WIDTH + POOL_WIDTH].reshape(nb, steps, POOL_WIDTH).transpose(1, 0, 2)
        yp_s_t, pool_s_t = _pool_mixer_batch(u_t, state_pool[l].transpose(1, 0, 2), past_len, pw, psc)
        yp_s = yp_s_t.transpose(1, 0, 2).reshape(n_s, POOL_WIDTH)

        cw_k = _compress_weights(cmp_pe[l, 0], cmp_w1[l, 0], cmp_w2[l, 0])
        cw_v = _compress_weights(cmp_pe[l, 1], cmp_w1[l, 1], cmp_w2[l, 1])
        kc_p = _compress_seq(kv[0][:n_p].reshape(n_p // CMP_STRIDE, CHUNK_FLAT), cw_k)
        vc_p = _compress_seq(kv[1][:n_p].reshape(n_p // CMP_STRIDE, CHUNK_FLAT), cw_v)
        tiles = lambda a: a[:n_p].astype(BF16).reshape(n_p // KEY_STEP, KEY_STEP, KV_WIDTH)
        yn_p = _nsa_seq(proj, gate_logits[:n_p].T, kc_p, vc_p.T, tiles(kv[2]), _value_steps(kv[3][:n_p]),
                        tiles(kv[4]), _value_steps(kv[5][:n_p]), n_p)

        kc_s = _compress_paged(kcmp_pages, l, page_table, cw_k)
        vc_s = _compress_paged(vcmp_pages, l, page_table, cw_v)
        gate_rows = gate_logits[n_p:].reshape(nb, steps, N_HEADS, 3).transpose(0, 2, 1, 3)
        gate_rows = jnp.pad(gate_rows.reshape(nb, N_HEADS * steps, 3), ((0, 0), (0, 0), (0, SUBLANES - 3)))
        yn_s, kw_s, vw_s = _nsa_batch(proj, n_p, steps, past_len, gate_rows, kc_s, vc_s, ksel_pages, vsel_pages, l,
                                      page_table, kwin_t[l], vwin_t[l])

        y_ssm = jnp.concatenate([ys_p, ys_s], axis=0)
        y_pool = jnp.concatenate([yp_p, yp_s], axis=0)
        y_nsa = jnp.concatenate([yn_p, yn_s.astype(BF16)], axis=0)
        x = _output_projection_ln(x, y_ssm, y_pool, y_nsa, w_out[l].astype(BF16), row(ln1_g[l]), row(ln1_b[l]))
        x = _mlp_ln(x, mlp_w1[l].astype(BF16), mlp_w2[l].astype(BF16), row(ln2_g[l]), row(ln2_b[l]))

        heads = lambda a, b_, t: a.reshape(b_, t, N_KV, HEAD_DIM)
        rows_first = lambda a: a.reshape(nb, N_KV, HEAD_DIM, WINDOW).transpose(0, 3, 1, 2)
        n_keep = min(WINDOW, lp)
        new_p.append([heads(kv[j][:n_p], bp, lp) for j in range(4)]
                     + [heads(kv[j][n_p - n_keep:n_p], bp, n_keep) for j in (4, 5)]
                     + [hr_p.reshape(bp, SSM_GROUPS, SSM_STATE), hi_p.reshape(bp, SSM_GROUPS, SSM_STATE),
                        pool_p[POOL_HIST - POOL_BUF:].reshape(bp, POOL_BUF, POOL_WIDTH)])
        new_s.append([heads(kv[j][n_p:], nb, steps) for j in range(4)]
                     + [rows_first(kw_s), rows_first(vw_s)]
                     + [hr_s.reshape(nb, SSM_GROUPS, SSM_STATE), hi_s.reshape(nb, SSM_GROUPS, SSM_STATE),
                        pool_s_t.transpose(1, 0, 2)])

    st_p = [jnp.stack(f) for f in zip(*new_p)]
    st_s = [jnp.stack(f) for f in zip(*new_s)]
    out = [x[:n_p].reshape(bp, lp, d), x[n_p:].reshape(nb, steps, d)]
    for a, b_ in zip(st_p, st_s):
        out += [a, b_]
    return tuple(out)
```

```python
import functools
import math

import jax
import jax.numpy as jnp
from jax import lax
from jax.experimental import pallas as pl
from jax.experimental.pallas import tpu as pltpu

F32 = jnp.float32
BF16 = jnp.bfloat16

D_MODEL = 2048
DEPTH = 2
PAGE_SIZE = 128
SSM_WIDTH = 512
SSM_GROUP_CH = 16
SSM_GROUPS = 32
SSM_STATE = 64
SSM_FLAT = SSM_GROUPS * SSM_STATE
POOL_WIDTH = 512
POOL_WINDOWS = (2, 4, 8, 16)
POOL_CH = 128
POOL_BUF = 15
NSA_WIDTH = 1024
HEAD_DIM = 64
N_HEADS = 16
N_KV = 4
GQA = 4
KV_WIDTH = N_KV * HEAD_DIM
CMP_STRIDE = 16
CMP_BLOCK = 32
CMP_HIDDEN = 128
SEL_BLOCK = 64
TOP_N = 16
WINDOW = 512
Q_BLOCK = 128
D_FF = 4 * D_MODEL
ALPHA = (2 * DEPTH) ** 0.25
LN_EPS = 1e-5
NEG_INF = -1e30
TINY = 1e-30
FORCED_SCORE = 1e4
PROJ_SIZES = (SSM_WIDTH, POOL_WIDTH, NSA_WIDTH) + (KV_WIDTH,) * 6 + (3 * N_HEADS,)
PROJ_WIDTH = sum(PROJ_SIZES)
PROJ_PAD = 3840
GATE_COL = 3584

LANES = 128
SUBLANES = 8
VMEM_LIMIT = 56 * 1024 * 1024


def _cparams(*sem):
    return pltpu.CompilerParams(dimension_semantics=sem, vmem_limit_bytes=VMEM_LIMIT)


def _gelu(x):
    return 0.5 * x * (1.0 + jnp.tanh(math.sqrt(2.0 / math.pi) * (x + 0.044715 * (x * x * x))))


def _sigmoid(x):
    return 1.0 / (1.0 + jnp.exp(-x))


def _layer_norm(z, g, b):
    zc = z - jnp.mean(z, axis=-1, keepdims=True)
    var = jnp.mean(zc * zc, axis=-1, keepdims=True)
    return zc * lax.rsqrt(var + LN_EPS) * g + b


def _proj_kernel(x_ref, w_ref, o_ref, xb_ref):
    @pl.when(pl.program_id(1) == 0)
    def _():
        xb_ref[...] = x_ref[...].astype(BF16)

    o_ref[...] = jnp.dot(xb_ref[...], w_ref[...].astype(BF16), preferred_element_type=F32)


def _input_projection(x, w, tm=1024, tn=768):
    m, k = x.shape
    n = w.shape[1]
    return pl.pallas_call(
        _proj_kernel,
        grid=(m // tm, n // tn),
        in_specs=[pl.BlockSpec((tm, k), lambda i, j: (i, 0)),
                  pl.BlockSpec((k, tn), lambda i, j: (0, j))],
        out_specs=pl.BlockSpec((tm, tn), lambda i, j: (i, j)),
        out_shape=jax.ShapeDtypeStruct((m, n), F32),
        scratch_shapes=[pltpu.VMEM((tm, k), BF16)],
        compiler_params=_cparams("parallel", "arbitrary"),
        name="input_projection",
    )(x, w)


def _outproj_ln_kernel(x_ref, ys_ref, yp_ref, yn_ref, w_ref, g_ref, b_ref, o_ref, wb_ref):
    @pl.when(pl.program_id(0) == 0)
    def _():
        wb_ref[...] = w_ref[...].astype(BF16)

    acc = jnp.dot(ys_ref[...], wb_ref[0:SSM_WIDTH, :], preferred_element_type=F32)
    acc += jnp.dot(yp_ref[...], wb_ref[SSM_WIDTH:SSM_WIDTH + POOL_WIDTH, :], preferred_element_type=F32)
    acc += jnp.dot(yn_ref[...], wb_ref[SSM_WIDTH + POOL_WIDTH:, :], preferred_element_type=F32)
    o_ref[...] = _layer_norm(ALPHA * x_ref[...] + acc, g_ref[...], b_ref[...])


def _output_projection_ln(x, y_ssm, y_pool, y_nsa, w, g, b, tm=512):
    m, d = x.shape
    row = lambda i: (i, 0)
    fixed = lambda i: (0, 0)
    return pl.pallas_call(
        _outproj_ln_kernel,
        grid=(m // tm,),
        in_specs=[pl.BlockSpec((tm, d), row),
                  pl.BlockSpec((tm, SSM_WIDTH), row),
                  pl.BlockSpec((tm, POOL_WIDTH), row),
                  pl.BlockSpec((tm, NSA_WIDTH), row),
                  pl.BlockSpec((d, d), fixed, pipeline_mode=pl.Buffered(1)),
                  pl.BlockSpec((1, d), fixed),
                  pl.BlockSpec((1, d), fixed)],
        out_specs=pl.BlockSpec((tm, d), row),
        out_shape=jax.ShapeDtypeStruct((m, d), F32),
        scratch_shapes=[pltpu.VMEM((d, d), BF16)],
        compiler_params=_cparams("arbitrary"),
        name="output_projection_ln",
    )(x, y_ssm, y_pool, y_nsa, w, g, b)


def _mlp_ln_kernel(x_ref, w1_ref, w2_ref, g_ref, b_ref, o_ref, xb_ref, acc_ref):
    f = pl.program_id(1)

    @pl.when(f == 0)
    def _():
        xb_ref[...] = x_ref[...].astype(BF16)
        acc_ref[...] = jnp.zeros_like(acc_ref)

    h = jnp.dot(xb_ref[...], w1_ref[...].astype(BF16), preferred_element_type=F32)
    h = jnp.square(jnp.maximum(h, 0.0)).astype(BF16)
    acc_ref[...] += jnp.dot(h, w2_ref[...].astype(BF16), preferred_element_type=F32)

    @pl.when(f == pl.num_programs(1) - 1)
    def _():
        o_ref[...] = _layer_norm(ALPHA * x_ref[...] + acc_ref[...], g_ref[...], b_ref[...])


def _mlp_ln(x, w1, w2, g, b, tm=1024, tf=512):
    m, d = x.shape
    ff = w1.shape[1]
    once = pl.Buffered(1)
    return pl.pallas_call(
        _mlp_ln_kernel,
        grid=(m // tm, ff // tf),
        in_specs=[pl.BlockSpec((tm, d), lambda i, f: (i, 0), pipeline_mode=once),
                  pl.BlockSpec((d, tf), lambda i, f: (0, f)),
                  pl.BlockSpec((tf, d), lambda i, f: (f, 0)),
                  pl.BlockSpec((1, d), lambda i, f: (0, 0)),
                  pl.BlockSpec((1, d), lambda i, f: (0, 0))],
        out_specs=pl.BlockSpec((tm, d), lambda i, f: (i, 0), pipeline_mode=once),
        out_shape=jax.ShapeDtypeStruct((m, d), F32),
        scratch_shapes=[pltpu.VMEM((tm, d), BF16), pltpu.VMEM((tm, d), F32)],
        compiler_params=_cparams("parallel", "arbitrary"),
        name="mlp_ln",
    )(x, w1, w2, g, b)


S5_GROUP_BLOCKS = 4
S5_BLOCK_LANES = SSM_FLAT // S5_GROUP_BLOCKS // LANES
S5_SLABS = SSM_FLAT // LANES


def _s5_kernel(u_ref, bre_ref, bim_ref, cre_ref, cim_ref, vec_ref, d_ref, wglu_ref, h0re_ref, h0im_ref,
               y_ref, hre_out, him_out, xre_ref, xim_ref, hre_s, him_s, *, n_seq, steps, carry):
    u = u_ref[...]
    ub = u.astype(BF16)
    cw = SSM_WIDTH // S5_GROUP_BLOCKS
    sw = SSM_FLAT // S5_GROUP_BLOCKS
    for j in range(S5_GROUP_BLOCKS):
        uj = ub[:, j * cw:(j + 1) * cw]
        bur = jnp.dot(uj, bre_ref[j], preferred_element_type=F32)
        bui = jnp.dot(uj, bim_ref[j], preferred_element_type=F32)
        zr = vec_ref[2:3, j * sw:(j + 1) * sw]
        zi = vec_ref[3:4, j * sw:(j + 1) * sw]
        xr = zr * bur - zi * bui
        xi = zr * bui + zi * bur
        for q in range(S5_BLOCK_LANES):
            xre_ref[j * S5_BLOCK_LANES + q] = xr[:, q * LANES:(q + 1) * LANES]
            xim_ref[j * S5_BLOCK_LANES + q] = xi[:, q * LANES:(q + 1) * LANES]

    if carry:
        @pl.when(pl.program_id(0) == 0)
        def _():
            hre_s[...] = h0re_ref[...]
            him_s[...] = h0im_ref[...]
    else:
        hre_s[...] = h0re_ref[...]
        him_s[...] = h0im_ref[...]

    for j in range(S5_GROUP_BLOCKS):
        slabs = [j * S5_BLOCK_LANES + q for q in range(S5_BLOCK_LANES)]
        lanes = [slice(s * LANES, (s + 1) * LANES) for s in slabs]
        ar = [jnp.broadcast_to(vec_ref[0:1, l], (n_seq, LANES)) for l in lanes]
        ai = [jnp.broadcast_to(vec_ref[1:2, l], (n_seq, LANES)) for l in lanes]

        def step(t, h, slabs=slabs, ar=ar, ai=ai):
            rows = pl.ds(t, n_seq, stride=steps) if n_seq > 1 else pl.ds(t, 1)
            out = []
            for q, s in enumerate(slabs):
                hr, hi = h[2 * q], h[2 * q + 1]
                nr = ar[q] * hr - ai[q] * hi + xre_ref[s, rows, :]
                ni = ar[q] * hi + ai[q] * hr + xim_ref[s, rows, :]
                xre_ref[s, rows, :] = nr
                xim_ref[s, rows, :] = ni
                out += [nr, ni]
            return tuple(out)

        h0 = []
        for l in lanes:
            h0 += [hre_s[:, l], him_s[:, l]]
        hT = lax.fori_loop(0, steps, step, tuple(h0), unroll=8)
        for q, l in enumerate(lanes):
            hre_s[:, l] = hT[2 * q]
            him_s[:, l] = hT[2 * q + 1]

    hre_out[...] = hre_s[...]
    him_out[...] = him_s[...]

    ys = []
    for j in range(S5_GROUP_BLOCKS):
        slabs = range(j * S5_BLOCK_LANES, (j + 1) * S5_BLOCK_LANES)
        hr = jnp.concatenate([xre_ref[s] for s in slabs], axis=-1).astype(BF16)
        hi = jnp.concatenate([xim_ref[s] for s in slabs], axis=-1).astype(BF16)
        ys.append(jnp.dot(hr, cre_ref[j], preferred_element_type=F32)
                  - jnp.dot(hi, cim_ref[j], preferred_element_type=F32))
    y = jnp.concatenate(ys, axis=-1) + d_ref[...] * u
    z = _gelu(y)
    gate = _sigmoid(jnp.dot(z.astype(BF16), wglu_ref[...], preferred_element_type=F32))
    y_ref[...] = (z * gate).astype(BF16)


def _s5_mixer(proj, row0, n_rows, n_seq, steps, carry, wts, h0_re, h0_im):
    bre, bim, cre, cim, vec, d, wglu = wts
    chunk = n_seq * steps
    n_chunks = n_rows // chunk
    blk0 = row0 // chunk
    st_rows = h0_re.shape[0]
    st_map = (lambda i: (0, 0)) if carry else (lambda i: (i, 0))
    fixed2 = lambda i: (0, 0)
    fixed3 = lambda i: (0, 0, 0)
    kern = functools.partial(_s5_kernel, n_seq=n_seq, steps=steps, carry=carry)
    return pl.pallas_call(
        kern,
        grid=(n_chunks,),
        in_specs=[pl.BlockSpec((chunk, SSM_WIDTH), lambda i: (blk0 + i, 0)),
                  pl.BlockSpec(bre.shape, fixed3), pl.BlockSpec(bim.shape, fixed3),
                  pl.BlockSpec(cre.shape, fixed3), pl.BlockSpec(cim.shape, fixed3),
                  pl.BlockSpec(vec.shape, fixed2), pl.BlockSpec(d.shape, fixed2),
                  pl.BlockSpec(wglu.shape, fixed2),
                  pl.BlockSpec((n_seq, SSM_FLAT), st_map), pl.BlockSpec((n_seq, SSM_FLAT), st_map)],
        out_specs=[pl.BlockSpec((chunk, SSM_WIDTH), lambda i: (i, 0)),
                   pl.BlockSpec((n_seq, SSM_FLAT), st_map), pl.BlockSpec((n_seq, SSM_FLAT), st_map)],
        out_shape=[jax.ShapeDtypeStruct((n_rows, SSM_WIDTH), BF16),
                   jax.ShapeDtypeStruct((st_rows, SSM_FLAT), F32),
                   jax.ShapeDtypeStruct((st_rows, SSM_FLAT), F32)],
        scratch_shapes=[pltpu.VMEM((S5_SLABS, chunk, LANES), F32), pltpu.VMEM((S5_SLABS, chunk, LANES), F32),
                        pltpu.VMEM((n_seq, SSM_FLAT), F32), pltpu.VMEM((n_seq, SSM_FLAT), F32)],
        compiler_params=_cparams("arbitrary"),
        name="s5_mixer_carry" if carry else "s5_mixer_batch",
    )(proj, bre, bim, cre, cim, vec, d, wglu, h0_re, h0_im)


def _s5_weights(a_re, a_im, log_dt, b_re, b_im, c_re, c_im, d, w_glu):
    dt = jnp.exp(log_dt)[:, None]
    mag = jnp.exp(a_re * dt)
    abar_re, abar_im = mag * jnp.cos(a_im * dt), mag * jnp.sin(a_im * dt)
    den = a_re * a_re + a_im * a_im
    zr = ((abar_re - 1.0) * a_re + abar_im * a_im) / den
    zi = (abar_im * a_re - (abar_re - 1.0) * a_im) / den
    flat = lambda v: v.reshape(1, SSM_FLAT)
    vec = jnp.concatenate([flat(abar_re), flat(abar_im), flat(zr), flat(zi),
                           jnp.zeros((SUBLANES - 4, SSM_FLAT), F32)], axis=0)
    gb = SSM_GROUPS // S5_GROUP_BLOCKS
    eye = jnp.eye(gb, dtype=F32)

    def pack_b(b):
        bb = b.reshape(S5_GROUP_BLOCKS, gb, SSM_STATE, SSM_GROUP_CH)
        m = jnp.einsum('jgph,gk->jghkp', bb, eye)
        return m.reshape(S5_GROUP_BLOCKS, gb * SSM_GROUP_CH, gb * SSM_STATE).astype(BF16)

    def pack_c(c):
        cc = c.reshape(S5_GROUP_BLOCKS, gb, SSM_GROUP_CH, SSM_STATE)
        m = jnp.einsum('jghp,gk->jgpkh', cc, eye)
        return m.reshape(S5_GROUP_BLOCKS, gb * SSM_STATE, gb * SSM_GROUP_CH).astype(BF16)

    return (pack_b(b_re), pack_b(b_im), pack_c(c_re), pack_c(c_im), vec, d.reshape(1, SSM_WIDTH),
            w_glu.astype(BF16))


POOL_HIST = 16


def _pool_seq_kernel(u_ref, buf_ref, w_ref, scale_ref, y_ref, new_ref, xc_ref, *, chunk, pos0):
    i = pl.program_id(0)

    @pl.when(i == 0)
    def _():
        xc_ref[0:POOL_HIST, :] = buf_ref[...]

    u = u_ref[...]
    xc_ref[POOL_HIST:POOL_HIST + chunk, :] = u
    pos = pos0 + i * chunk + lax.broadcasted_iota(jnp.int32, (chunk, POOL_CH), 0)
    outs = []
    for g, wd in enumerate(POOL_WINDOWS):
        lanes = slice(g * POOL_CH, (g + 1) * POOL_CH)
        ug = u[:, lanes]
        acc = ug
        for k in range(1, wd):
            acc = acc + xc_ref[pl.ds(POOL_HIST - k, chunk), lanes]
        mix = acc / jnp.minimum(wd, pos + 1).astype(F32) - ug
        outs.append(jnp.dot(mix.astype(BF16), w_ref[g], preferred_element_type=F32))
    y_ref[...] = (jnp.concatenate(outs, axis=-1) * scale_ref[...]).astype(BF16)
    tail = xc_ref[chunk:chunk + POOL_HIST, :]
    xc_ref[0:POOL_HIST, :] = tail
    new_ref[...] = tail


def _pool_mixer_seq(proj, row0, n_rows, pos0, buf16, w, scale, chunk=256):
    blk0 = row0 // chunk
    kern = functools.partial(_pool_seq_kernel, chunk=chunk, pos0=pos0)
    return pl.pallas_call(
        kern,
        grid=(n_rows // chunk,),
        in_specs=[pl.BlockSpec((chunk, POOL_WIDTH), lambda i: (blk0 + i, 1)),
                  pl.BlockSpec((POOL_HIST, POOL_WIDTH), lambda i: (0, 0)),
                  pl.BlockSpec(w.shape, lambda i: (0, 0, 0)),
                  pl.BlockSpec((1, POOL_WIDTH), lambda i: (0, 0))],
        out_specs=[pl.BlockSpec((chunk, POOL_WIDTH), lambda i: (i, 0)),
                   pl.BlockSpec((POOL_HIST, POOL_WIDTH), lambda i: (0, 0))],
        out_shape=[jax.ShapeDtypeStruct((n_rows, POOL_WIDTH), BF16),
                   jax.ShapeDtypeStruct((POOL_HIST, POOL_WIDTH), F32)],
        scratch_shapes=[pltpu.VMEM((POOL_HIST + chunk, POOL_WIDTH), F32)],
        compiler_params=_cparams("arbitrary"),
        name="pool_mixer_seq",
    )(proj, buf16, w, scale)


def _pool_batch_kernel(u_ref, buf_ref, w_ref, scale_ref, y_ref, new_ref, *, steps, pos0):
    def xrow(j, lanes):
        return buf_ref[j, :, lanes] if j < POOL_BUF else u_ref[j - POOL_BUF, :, lanes]

    nb = u_ref.shape[1]
    for g, wd in enumerate(POOL_WINDOWS):
        lanes = slice(g * POOL_CH, (g + 1) * POOL_CH)
        mixes = []
        for t in range(steps):
            ug = u_ref[t, :, lanes]
            acc = ug
            for k in range(1, wd):
                acc = acc + xrow(POOL_BUF + t - k, lanes)
            mixes.append(acc / float(min(wd, pos0 + t + 1)) - ug)
        mix = jnp.concatenate(mixes, axis=0).astype(BF16)
        yg = jnp.dot(mix, w_ref[g], preferred_element_type=F32) * scale_ref[:, lanes]
        for t in range(steps):
            y_ref[t, :, lanes] = yg[t * nb:(t + 1) * nb].astype(BF16)
    for j in range(POOL_BUF):
        new_ref[j] = xrow(steps + j, slice(None))


def _pool_mixer_batch(u_t, buf_t, pos0, w, scale):
    steps, nb, _ = u_t.shape
    kern = functools.partial(_pool_batch_kernel, steps=steps, pos0=pos0)
    full3 = lambda i: (0, 0, 0)
    return pl.pallas_call(
        kern,
        grid=(1,),
        in_specs=[pl.BlockSpec(u_t.shape, full3), pl.BlockSpec(buf_t.shape, full3),
                  pl.BlockSpec(w.shape, full3), pl.BlockSpec((1, POOL_WIDTH), lambda i: (0, 0))],
        out_specs=[pl.BlockSpec(u_t.shape, full3), pl.BlockSpec(buf_t.shape, full3)],
        out_shape=[jax.ShapeDtypeStruct(u_t.shape, BF16), jax.ShapeDtypeStruct(buf_t.shape, F32)],
        compiler_params=_cparams("arbitrary"),
        name="pool_mixer_batch",
    )(u_t, buf_t, w, scale)


CMP_HALVES = KV_WIDTH // LANES
CMP_PAIR = LANES // HEAD_DIM
CMP_K = CMP_STRIDE * LANES
CMP_N = 2 * CMP_PAIR * CMP_HIDDEN


def _compress_kernel(*refs, n_seq, n_blk, transposed):
    if transposed:
        refs = refs[1:]
    blk_refs = refs[:n_seq * n_blk]
    wpair_ref, pe_ref, w1_ref, w2_ref, out_ref, xs_ref, g_ref, b_ref = refs[n_seq * n_blk:]
    blk_rows = blk_refs[0].shape[1] if transposed else blk_refs[0].shape[0]
    n = n_blk * blk_rows // CMP_STRIDE
    total = n_seq * n
    for i, r in enumerate(blk_refs):
        x = r[...].T if transposed else r[...]
        for h in range(CMP_HALVES):
            xs_ref[h, i * blk_rows:(i + 1) * blk_rows, :] = x[:, h * LANES:(h + 1) * LANES]
    ab = []
    for h in range(CMP_HALVES):
        for s_ in range(CMP_STRIDE):
            g_ref[h, :, s_ * LANES:(s_ + 1) * LANES] = (
                xs_ref[h, pl.ds(s_, total, stride=CMP_STRIDE), :].astype(BF16))
        ab.append(jnp.dot(g_ref[h], wpair_ref[...], preferred_element_type=F32))
    hid0 = jnp.dot(pe_ref[...], w1_ref[...], preferred_element_type=F32)[0:1]
    hid0 = jnp.concatenate([hid0] * CMP_PAIR, axis=-1)
    half_n = CMP_N // 2
    row = lax.broadcasted_iota(jnp.int32, (n, LANES), 0)
    b_ref[n:n + SUBLANES, :] = jnp.zeros((SUBLANES, half_n), F32)
    for h in range(CMP_HALVES):
        for q in range(n_seq):
            b_ref[0:n, :] = ab[h][q * n:(q + 1) * n, half_n:]
            hid = ab[h][q * n:(q + 1) * n, :half_n] + b_ref[pl.ds(1, n), :] + hid0
            out = jnp.dot(_gelu(hid).astype(BF16), w2_ref[...], preferred_element_type=F32)
            out_ref[q, :, h * LANES:(h + 1) * LANES] = jnp.where(row < n - 1, out, 0.0).astype(out_ref.dtype)


def _compress_weights(pe, w1, w2):
    w1b = w1.reshape(2, CMP_STRIDE, HEAD_DIM, CMP_HIDDEN)
    eye = jnp.eye(CMP_PAIR, dtype=F32)
    wpair = jnp.einsum('asdh,kj->skdajh', w1b, eye).reshape(CMP_K, CMP_N).astype(BF16)
    w2pair = jnp.einsum('hd,kj->khjd', w2, eye).reshape(CMP_PAIR * CMP_HIDDEN, LANES).astype(BF16)
    pe8 = jnp.concatenate([pe.reshape(1, CMP_BLOCK * HEAD_DIM),
                           jnp.zeros((SUBLANES - 1, CMP_BLOCK * HEAD_DIM), F32)], axis=0).astype(BF16)
    return wpair, pe8, w1.astype(BF16), w2pair


def _compress_scratch(n_seq, rows, n):
    return [pltpu.VMEM((CMP_HALVES, n_seq * rows, LANES), F32),
            pltpu.VMEM((CMP_HALVES, n_seq * n, CMP_K), BF16),
            pltpu.VMEM((n + SUBLANES, CMP_N // 2), F32)]


def _compress_seq(proj, col_block, n_rows, wts):
    n = n_rows // CMP_STRIDE
    full = lambda i: (0, 0)
    return pl.pallas_call(
        functools.partial(_compress_kernel, n_seq=1, n_blk=1, transposed=False),
        grid=(1,),
        in_specs=[pl.BlockSpec((n_rows, KV_WIDTH), lambda i: (0, col_block))]
        + [pl.BlockSpec(w.shape, full) for w in wts],
        out_specs=pl.BlockSpec((1, n, KV_WIDTH), lambda i: (0, 0, 0)),
        out_shape=jax.ShapeDtypeStruct((1, n, KV_WIDTH), BF16),
        scratch_shapes=_compress_scratch(1, n_rows, n),
        compiler_params=_cparams("arbitrary"),
        name="compress_seq",
    )(proj, *wts)[0]


def _compress_paged(cache_t, layer, page_table, wts, group=2):
    nb, n_pages = page_table.shape
    n = n_pages * PAGE_SIZE // CMP_STRIDE
    full = lambda b, pt: (0, 0)

    def page_spec(bl, p):
        return pl.BlockSpec((None, None, KV_WIDTH, PAGE_SIZE), lambda b, pt: (layer, pt[group * b + bl, p], 0, 0))

    grid_spec = pltpu.PrefetchScalarGridSpec(
        num_scalar_prefetch=1,
        grid=(nb // group,),
        in_specs=[page_spec(bl, p) for bl in range(group) for p in range(n_pages)]
        + [pl.BlockSpec(w.shape, full) for w in wts],
        out_specs=pl.BlockSpec((group, n, KV_WIDTH), lambda b, pt: (b, 0, 0)),
        scratch_shapes=_compress_scratch(group, n_pages * PAGE_SIZE, n),
    )
    return pl.pallas_call(
        functools.partial(_compress_kernel, n_seq=group, n_blk=n_pages, transposed=True),
        grid_spec=grid_spec,
        out_shape=jax.ShapeDtypeStruct((nb, n, KV_WIDTH), BF16),
        compiler_params=_cparams("arbitrary"),
        name="compress_paged",
    )(page_table, *([cache_t] * (group * n_pages)), *wts)


def _alibi_slope(h):
    return 2.0 ** (-8.0 * (h + 1) / N_HEADS)


NT_DIMS = (((1,), (1,)), ((), ()))
QROWS = GQA * Q_BLOCK
KEY_TILE = 128
WIN_TILES = WINDOW // KEY_TILE
KEY_STEP = 2 * KEY_TILE
T_FULL, T_DIAG, T_OLD, T_NONE = 0, 1, 2, 3
LOG2E = math.log2(math.e)
V_ROWS = HEAD_DIM + 16


def _nsa_seq_kernel(q_ref, gate_ref, kc_ref, vct_ref, ks_ref, vst_ref, kw_ref, vwt_ref, o_ref,
                    qx_ref, bias_ref, sbuf_ref, m_ref, acc_ref, oc_ref, os_ref, ow_ref, *, n_cmp_pad, n_sel, n_steps):
    i = pl.program_id(0)
    base = i * Q_BLOCK

    lane = lax.broadcasted_iota(jnp.int32, (1, QROWS), 1)
    qq_i = lane % Q_BLOCK
    qq = qq_i.astype(F32)
    g_lane = lane // Q_BLOCK

    def slope_row(k):
        r = jnp.full((1, QROWS), LOG2E * _alibi_slope(GQA * k + GQA - 1), F32)
        for g in range(GQA - 1):
            r = jnp.where(g_lane == g, LOG2E * _alibi_slope(GQA * k + g), r)
        return r

    slopes = [slope_row(k) for k in range(N_KV)]

    @pl.when(i == 0)
    def _():
        kk = lax.broadcasted_iota(jnp.int32, (KEY_TILE, QROWS), 0).astype(F32)
        for half in range(2):
            for k in range(N_KV):
                b = -slopes[k] * (qq - (kk + float(half * KEY_TILE)))
                bias_ref[half, T_FULL, k] = b
                bias_ref[half, T_DIAG, k] = jnp.where(kk <= qq, b, NEG_INF)
                bias_ref[half, T_OLD, k] = jnp.where(kk > qq, b, NEG_INF)
                bias_ref[half, T_NONE, k] = jnp.full((KEY_TILE, QROWS), NEG_INF, F32)

    qs = q_ref[...] * (LOG2E * HEAD_DIM ** -0.5)
    low_half = lax.broadcasted_iota(jnp.int32, (Q_BLOCK, LANES), 1) < HEAD_DIM
    for k in range(N_KV):
        keep = low_half if k % 2 == 0 else jnp.logical_not(low_half)
        parts = []
        for g in range(GQA):
            h = GQA * k + g
            pair = qs[:, (h // 2) * LANES:(h // 2 + 1) * LANES]
            if h % 2 != k % 2:
                pair = pltpu.roll(pair, HEAD_DIM, 1)
            parts.append(jnp.where(keep, pair, 0.0).astype(BF16))
        qx_ref[k, :, 0:LANES] = jnp.concatenate(parts, axis=0)

    n_io = lax.broadcasted_iota(jnp.int32, (n_cmp_pad, QROWS), 0)
    distc = (base.astype(F32) + qq) - (n_io.astype(F32) * CMP_STRIDE + (CMP_BLOCK - 1) / 2.0)
    visc = (n_io * CMP_STRIDE + (CMP_BLOCK - 1)) <= (base + qq_i)
    sj_o = lax.broadcasted_iota(jnp.int32, (n_sel, n_cmp_pad), 0)
    nn_o = lax.broadcasted_iota(jnp.int32, (n_sel, n_cmp_pad), 1)
    ovt = jnp.where((nn_o * CMP_STRIDE < (sj_o + 1) * SEL_BLOCK)
                    & (nn_o * CMP_STRIDE + (CMP_BLOCK - 1) >= sj_o * SEL_BLOCK), 1.0, 0.0).astype(BF16)
    sjf = lax.broadcasted_iota(jnp.int32, (n_sel, Q_BLOCK), 0).astype(F32)
    curf = ((base + lax.broadcasted_iota(jnp.int32, (1, Q_BLOCK), 1)) // SEL_BLOCK).astype(F32)
    visible = sjf <= curf
    forced = visible & ((sjf == 0.0) | (sjf == curf) | (sjf == curf - 1.0))

    for k in range(N_KV):
        kl = slice((k // 2) * LANES, (k // 2 + 1) * LANES)
        vr = slice(k * HEAD_DIM, (k + 1) * HEAD_DIM)
        s = lax.dot_general(kc_ref[:, kl], qx_ref[k, :, 0:LANES], NT_DIMS, preferred_element_type=F32)
        s = jnp.where(visc, s - slopes[k] * distc, NEG_INF)
        m = jnp.maximum(jnp.max(s, axis=0, keepdims=True), 0.5 * NEG_INF)
        p = jnp.exp2(s - m)
        p = p * (1.0 / jnp.maximum(jnp.sum(p, axis=0, keepdims=True), TINY))
        oc_ref[k] = jnp.dot(vct_ref[vr, :], p.astype(BF16), preferred_element_type=F32)
        psum = p[:, 0:Q_BLOCK]
        for g in range(1, GQA):
            psum = psum + p[:, g * Q_BLOCK:(g + 1) * Q_BLOCK]
        p_hi = psum.astype(BF16)
        p_lo = (psum - p_hi.astype(F32)).astype(BF16)
        imp = (jnp.dot(ovt, p_hi, preferred_element_type=F32)
               + jnp.dot(ovt, p_lo, preferred_element_type=F32))
        score = jnp.where(forced, FORCED_SCORE, jnp.where(visible, imp, -1.0))
        sel = jnp.zeros_like(score)
        for _ in range(min(TOP_N, n_sel)):
            mx = jnp.max(score, axis=0, keepdims=True)
            idx = jnp.min(jnp.where(score == mx, sjf, 1e9), axis=0, keepdims=True)
            hit = sjf == idx
            sel = jnp.where(hit & (mx >= 0.0), 1.0, sel)
            score = jnp.where(hit, -2.0, score)
        selq = jnp.where(sel > 0.0, 0.0, NEG_INF).T.astype(BF16)
        if n_sel < LANES:
            selq = jnp.concatenate([selq, jnp.zeros((Q_BLOCK, LANES - n_sel), BF16)], axis=1)
        qx_ref[k, :, LANES:2 * LANES] = jnp.concatenate([selq] * GQA, axis=0)

    blk_of_key = lax.broadcasted_iota(jnp.int32, (KEY_STEP, LANES), 0) // SEL_BLOCK
    lane_id = lax.broadcasted_iota(jnp.int32, (KEY_STEP, LANES), 1)

    def flash(k_ref, vt_ref, j_lo, use_sel, out_ref):
        m_ref[...] = jnp.full(m_ref.shape, NEG_INF, F32)
        acc_ref[...] = jnp.zeros(acc_ref.shape, F32)

        def tile_type(t):
            ty = jnp.where(t == i, T_DIAG, jnp.where(t > i, T_NONE, T_FULL))
            if not use_sel:
                ty = jnp.where(t == i - WIN_TILES, T_OLD, jnp.where(t < i - WIN_TILES, T_NONE, ty))
            return ty

        def scores(j, heads):
            jj = jnp.minimum(j, n_steps - 1)
            ty0, ty1 = tile_type(2 * jj), tile_type(2 * jj + 1)
            if use_sel:
                onehot = jnp.where(lane_id == blk_of_key + (KEY_STEP // SEL_BLOCK) * jj, 1.0, 0.0).astype(BF16)
            out = []
            for k in heads:
                kl = slice((k // 2) * LANES, (k // 2 + 1) * LANES)
                if use_sel:
                    kx = jnp.concatenate([k_ref[jj, :, kl], onehot], axis=1)
                    s = lax.dot_general(kx, qx_ref[k], NT_DIMS, preferred_element_type=F32)
                else:
                    s = lax.dot_general(k_ref[jj, :, kl], qx_ref[k, :, 0:LANES], NT_DIMS,
                                        preferred_element_type=F32)
                out.append(s + jnp.concatenate([bias_ref[0, ty0, k], bias_ref[1, ty1, k]], axis=0))
            return out

        def softmax_pv(j, k, s):
            c = slopes[k] * (i * KEY_TILE - j * KEY_STEP).astype(F32)
            m_old = m_ref[k] + c
            m_new = jnp.maximum(m_old, jnp.max(s, axis=0, keepdims=True))
            alpha = jnp.exp2(m_old - m_new)
            p = jnp.exp2(s - m_new).astype(BF16)
            m_ref[k] = m_new - c
            acc_ref[k] = alpha * acc_ref[k] + jnp.dot(vt_ref[j, k * V_ROWS:(k + 1) * V_ROWS, :], p,
                                                      preferred_element_type=F32)

        sbuf_ref[0], sbuf_ref[1] = scores(j_lo, (0, 1))

        def body(j, carry):
            (s2,) = scores(j, (2,))
            softmax_pv(j, 0, sbuf_ref[0])
            (s3,) = scores(j, (3,))
            softmax_pv(j, 1, sbuf_ref[1])
            n0, n1 = scores(j + 1, (0, 1))
            sbuf_ref[0] = n0
            softmax_pv(j, 2, s2)
            sbuf_ref[1] = n1
            softmax_pv(j, 3, s3)
            return carry

        lax.fori_loop(j_lo, i // 2 + 1, body, 0)
        for k in range(N_KV):
            out_ref[k] = acc_ref[k, 0:HEAD_DIM] * (1.0 / jnp.maximum(acc_ref[k, HEAD_DIM:HEAD_DIM + 1], TINY))

    flash(ks_ref, vst_ref, 0, True, os_ref)
    flash(kw_ref, vwt_ref, jnp.maximum(i - WIN_TILES, 0) // 2, False, ow_ref)

    gate = _sigmoid(gate_ref[...])
    blocks = []
    for k in range(N_KV):
        for g in range(GQA):
            h = GQA * k + g
            sl = slice(g * Q_BLOCK, (g + 1) * Q_BLOCK)
            blocks.append(oc_ref[k, :, sl] * gate[3 * h:3 * h + 1]
                          + os_ref[k, :, sl] * gate[3 * h + 1:3 * h + 2]
                          + ow_ref[k, :, sl] * gate[3 * h + 2:3 * h + 3])
    o_ref[...] = jnp.concatenate(blocks, axis=0).T.astype(BF16)


def _value_steps(v):
    steps = v.shape[0] // KEY_STEP
    vt = v.astype(BF16).reshape(steps, KEY_STEP, N_KV, HEAD_DIM).transpose(0, 2, 3, 1)
    ones = jnp.ones((steps, N_KV, 1, KEY_STEP), BF16)
    pad = jnp.zeros((steps, N_KV, V_ROWS - HEAD_DIM - 1, KEY_STEP), BF16)
    return jnp.concatenate([vt, ones, pad], axis=2).reshape(steps, N_KV * V_ROWS, KEY_STEP)


def _nsa_seq(proj, gate_t, kc, vct, ks_t, vst_t, kw_t, vwt_t, n_rows):
    n_tiles = n_rows // Q_BLOCK
    assert n_tiles % 2 == 0
    n_cmp_pad = kc.shape[0]
    n_sel = n_rows // SEL_BLOCK
    assert n_sel <= LANES
    kern = functools.partial(_nsa_seq_kernel, n_cmp_pad=n_cmp_pad, n_sel=n_sel, n_steps=n_tiles // 2)
    c2 = lambda i: (0, 0)
    c3 = lambda i: (0, 0, 0)
    st = (N_KV, HEAD_DIM, QROWS)
    return pl.pallas_call(
        kern,
        grid=(n_tiles,),
        in_specs=[pl.BlockSpec((Q_BLOCK, NSA_WIDTH), lambda i: (i, 1)),
                  pl.BlockSpec((3 * N_HEADS, Q_BLOCK), lambda i: (0, i)),
                  pl.BlockSpec(kc.shape, c2), pl.BlockSpec(vct.shape, c2),
                  pl.BlockSpec(ks_t.shape, c3), pl.BlockSpec(vst_t.shape, c3),
                  pl.BlockSpec(kw_t.shape, c3), pl.BlockSpec(vwt_t.shape, c3)],
        out_specs=pl.BlockSpec((Q_BLOCK, NSA_WIDTH), lambda i: (i, 0)),
        out_shape=jax.ShapeDtypeStruct((n_rows, NSA_WIDTH), BF16),
        scratch_shapes=[pltpu.VMEM((N_KV, QROWS, 2 * LANES), BF16),
                        pltpu.VMEM((2, 4, N_KV, KEY_TILE, QROWS), F32),
                        pltpu.VMEM((2, KEY_STEP, QROWS), F32),
                        pltpu.VMEM((N_KV, 1, QROWS), F32), pltpu.VMEM((N_KV, V_ROWS, QROWS), F32),
                        pltpu.VMEM(st, F32), pltpu.VMEM(st, F32), pltpu.VMEM(st, F32)],
        compiler_params=_cparams("arbitrary"),
        name="nsa_seq",
    )(proj, gate_t, kc, vct, ks_t, vst_t, kw_t, vwt_t)


def _nsa_batch_kernel(pt_ref, *refs, n_pages, steps, pos0, group):
    del pt_ref
    (kwin_ref, vwin_ref, kc_ref, vc_ref, q_ref, ksn_ref, vsn_ref, kwn_ref, vwn_ref, gate_ref,
     o_ref, kwo_ref, vwo_ref, bsel_ref, bwin_ref, bcmp_ref, e_ref, ov_ref) = refs[2 * group * n_pages:]
    rows = N_HEADS * steps
    past = n_pages * PAGE_SIZE
    nk = past + KEY_TILE
    nw = WINDOW + KEY_TILE

    @pl.when(pl.program_id(0) == 0)
    def _():
        def tables(width):
            r = lax.broadcasted_iota(jnp.int32, (rows, width), 0)
            c = lax.broadcasted_iota(jnp.int32, (rows, width), 1)
            slope = jnp.exp((-8.0 * math.log(2.0) / N_HEADS) * (r // steps + 1).astype(F32))
            return slope, pos0 + r % steps, c

        slope, qpos, key = tables(nk)
        bsel_ref[...] = jnp.where(key <= qpos, -slope * (qpos - key).astype(F32), NEG_INF)
        slope, qpos, w = tables(nw)
        kpos = pos0 - WINDOW + w
        d = qpos - kpos
        bwin_ref[...] = jnp.where((d >= 0) & (d < WINDOW) & (kpos >= 0), -slope * d.astype(F32), NEG_INF)
        slope, qpos, n = tables(LANES)
        c_mid = n.astype(F32) * CMP_STRIDE + (CMP_BLOCK - 1) / 2.0
        bcmp_ref[...] = jnp.where(n * CMP_STRIDE + (CMP_BLOCK - 1) <= qpos,
                                  -slope * (qpos.astype(F32) - c_mid), NEG_INF)
        sj = lax.broadcasted_iota(jnp.int32, (LANES, nk), 0)
        key = lax.broadcasted_iota(jnp.int32, (LANES, nk), 1)
        e_ref[...] = jnp.where(key // SEL_BLOCK == sj, 1.0, 0.0).astype(BF16)
        n = lax.broadcasted_iota(jnp.int32, (LANES, LANES), 0)
        sj = lax.broadcasted_iota(jnp.int32, (LANES, LANES), 1)
        ov_ref[...] = jnp.where((n * CMP_STRIDE < (sj + 1) * SEL_BLOCK)
                                & (n * CMP_STRIDE + (CMP_BLOCK - 1) >= sj * SEL_BLOCK), 1.0, 0.0).astype(BF16)

    def _nsa_batch_one(bl):
        ks_pages = refs[bl * n_pages:(bl + 1) * n_pages]
        vs_pages = refs[(group + bl) * n_pages:(group + bl + 1) * n_pages]
        tok = slice(bl * steps, (bl + 1) * steps)
        qs = q_ref[tok, :] * (HEAD_DIM ** -0.5)
        low_half = lax.broadcasted_iota(jnp.int32, (steps, LANES), 1) < HEAD_DIM
        zero = jnp.zeros((steps, LANES), F32)
        pieces = []
        for k in range(N_KV):
            keep = low_half if k % 2 == 0 else jnp.logical_not(low_half)
            for g in range(GQA):
                h = GQA * k + g
                pair = qs[:, (h // 2) * LANES:(h // 2 + 1) * LANES]
                if h % 2 != k % 2:
                    pair = pltpu.roll(pair, HEAD_DIM, 1)
                blk = jnp.where(keep, pair, 0.0)
                pieces.append(jnp.concatenate([blk, zero] if k // 2 == 0 else [zero, blk], axis=1))
        qb = jnp.concatenate(pieces, axis=0).astype(BF16)

        def new_tile(ref):
            return jnp.concatenate([ref[tok, :], jnp.zeros((KEY_TILE - steps, KV_WIDTH), F32)], axis=0).astype(BF16)

        def softmax_pv(s, v_parts, transposed):
            m = jnp.max(s, axis=-1, keepdims=True)
            p = jnp.exp(s - m)
            l = jnp.sum(p, axis=-1, keepdims=True)
            acc = None
            off = 0
            for v, v_t in zip(v_parts, transposed):
                n = v.shape[1] if v_t else v.shape[0]
                pb = p[:, off:off + n].astype(BF16)
                part = (lax.dot_general(pb, v, NT_DIMS, preferred_element_type=F32) if v_t
                        else jnp.dot(pb, v, preferred_element_type=F32))
                acc = part if acc is None else acc + part
                off += n
            return acc * (1.0 / jnp.maximum(l, TINY))

        bc = bcmp_ref[...]
        s_c = lax.dot_general(qb, kc_ref[bl], NT_DIMS, preferred_element_type=F32) + bc
        kt_all = jnp.concatenate([r[...] for r in ks_pages], axis=1).astype(BF16)
        s_s = jnp.concatenate([jnp.dot(qb, kt_all, preferred_element_type=F32),
                               lax.dot_general(qb, new_tile(ksn_ref), NT_DIMS, preferred_element_type=F32)],
                              axis=1) + bsel_ref[...]
        s_w = jnp.concatenate([jnp.dot(qb, kwin_ref[bl].astype(BF16), preferred_element_type=F32),
                               lax.dot_general(qb, new_tile(kwn_ref), NT_DIMS, preferred_element_type=F32)],
                              axis=1) + bwin_ref[...]
        yield

        m = jnp.max(s_c, axis=-1, keepdims=True)
        p = jnp.exp(s_c - m) * jnp.where(bc > 0.5 * NEG_INF, 1.0, 0.0)
        p = p * (1.0 / jnp.maximum(jnp.sum(p, axis=-1, keepdims=True), TINY))
        o_c = jnp.dot(p.astype(BF16), vc_ref[bl], preferred_element_type=F32)
        grp = GQA * steps
        psum = []
        for k in range(N_KV):
            acc = p[k * grp:k * grp + steps]
            for g in range(1, GQA):
                acc = acc + p[k * grp + g * steps:k * grp + (g + 1) * steps]
            psum.append(acc)
        psum = jnp.concatenate(psum, axis=0)
        p_hi = psum.astype(BF16)
        p_lo = (psum - p_hi.astype(F32)).astype(BF16)
        imp = (jnp.dot(p_hi, ov_ref[...], preferred_element_type=F32)
               + jnp.dot(p_lo, ov_ref[...], preferred_element_type=F32))
        yield

        o_w = softmax_pv(s_w, [vwin_ref[bl].astype(BF16), new_tile(vwn_ref)], [True, False])
        lane_w = lax.broadcasted_iota(jnp.int32, (KV_WIDTH, WINDOW), 1)
        for src, nw_ref, dst in ((kwin_ref, kwn_ref, kwo_ref), (vwin_ref, vwn_ref, vwo_ref)):
            new_t = jnp.concatenate([nw_ref[tok, :], jnp.zeros((LANES - steps, KV_WIDTH), F32)], axis=0).T
            new_t = jnp.concatenate([new_t] * (WINDOW // LANES), axis=1)
            shifted = pltpu.roll(src[bl], WINDOW - steps, 1)
            dst[bl] = jnp.where(lane_w < WINDOW - steps, shifted, pltpu.roll(new_t, WINDOW - steps, 1))
        yield

        sjf = lax.broadcasted_iota(jnp.int32, imp.shape, 1).astype(F32)
        step_of_row = lax.broadcasted_iota(jnp.int32, imp.shape, 0) % steps
        curf = ((pos0 + step_of_row) // SEL_BLOCK).astype(F32)
        visible = sjf <= curf
        forced = visible & ((sjf == 0.0) | (sjf == curf) | (sjf == curf - 1.0))
        score = jnp.where(forced, FORCED_SCORE, jnp.where(visible, imp, -1.0))
        n_blocks = -(-(past + steps) // SEL_BLOCK)
        rank = jnp.zeros_like(score)
        for jp in range(n_blocks):
            other = jnp.broadcast_to(score[:, jp:jp + 1], score.shape)
            beats = (other > score) | ((other == score) & (sjf > float(jp)))
            rank = rank + jnp.where(beats, 1.0, 0.0)
        sel = jnp.where((rank < float(TOP_N)) & (score >= 0.0), 1.0, 0.0)
        selk = jnp.dot(sel.astype(BF16), e_ref[...], preferred_element_type=F32)
        yield

        selb = jnp.where(selk > 0.5, 0.0, NEG_INF)
        sb = jnp.concatenate([selb[k * steps:(k + 1) * steps] for k in range(N_KV) for _ in range(GQA)], axis=0)
        vt_all = jnp.concatenate([r[...] for r in vs_pages], axis=1).astype(BF16)
        o_s = softmax_pv(s_s + sb, [vt_all, new_tile(vsn_ref)], [True, False])
        yield

        gates = _sigmoid(gate_ref[bl])
        y = o_c * gates[:, 0:1] + o_s * gates[:, 1:2] + o_w * gates[:, 2:3]
        for h in range(N_HEADS):
            k = h // GQA
            o_ref[tok, h * HEAD_DIM:(h + 1) * HEAD_DIM] = y[h * steps:(h + 1) * steps, k * HEAD_DIM:(k + 1) * HEAD_DIM]

    pending = [_nsa_batch_one(bl) for bl in range(group)]
    while pending:
        pending = [g for g in pending if next(g, True) is None]


def _nsa_batch(proj, row0, steps, pos0, gate_rows, kc, vc, ks_cache, vs_cache, layer, page_table, kwin, vwin,
               group=2):
    nb, n_pages = page_table.shape
    rows = N_HEADS * steps
    nk = n_pages * PAGE_SIZE + KEY_TILE
    tok = group * steps
    blk0 = row0 // tok
    kern = functools.partial(_nsa_batch_kernel, n_pages=n_pages, steps=steps, pos0=pos0, group=group)

    def page_spec(bl, p):
        return pl.BlockSpec((None, None, KV_WIDTH, PAGE_SIZE), lambda b, pt: (layer, pt[group * b + bl, p], 0, 0))

    pages = [page_spec(bl, p) for bl in range(group) for p in range(n_pages)]

    per_b = lambda b, pt: (b, 0, 0)
    kv_col0 = sum(PROJ_SIZES[:3]) // KV_WIDTH

    def new_spec(j):
        return pl.BlockSpec((tok, KV_WIDTH), lambda b, pt: (blk0 + b, kv_col0 + j))

    grid_spec = pltpu.PrefetchScalarGridSpec(
        num_scalar_prefetch=1,
        grid=(nb // group,),
        in_specs=pages + pages
        + [pl.BlockSpec((group, KV_WIDTH, WINDOW), per_b), pl.BlockSpec((group, KV_WIDTH, WINDOW), per_b),
           pl.BlockSpec((group,) + kc.shape[1:], per_b), pl.BlockSpec((group,) + vc.shape[1:], per_b),
           pl.BlockSpec((tok, NSA_WIDTH), lambda b, pt: (blk0 + b, 1)),
           new_spec(2), new_spec(3), new_spec(4), new_spec(5),
           pl.BlockSpec((group, rows, SUBLANES), per_b)],
        out_specs=[pl.BlockSpec((tok, NSA_WIDTH), lambda b, pt: (b, 0)),
                   pl.BlockSpec((group, KV_WIDTH, WINDOW), per_b), pl.BlockSpec((group, KV_WIDTH, WINDOW), per_b)],
        scratch_shapes=[pltpu.VMEM((rows, nk), F32), pltpu.VMEM((rows, WINDOW + KEY_TILE), F32),
                        pltpu.VMEM((rows, LANES), F32), pltpu.VMEM((LANES, nk), BF16),
                        pltpu.VMEM((LANES, LANES), BF16)],
    )
    return pl.pallas_call(
        kern,
        grid_spec=grid_spec,
        out_shape=[jax.ShapeDtypeStruct((nb * steps, NSA_WIDTH), F32),
                   jax.ShapeDtypeStruct((nb, KV_WIDTH, WINDOW), F32),
                   jax.ShapeDtypeStruct((nb, KV_WIDTH, WINDOW), F32)],
        compiler_params=_cparams("arbitrary"),
        name="nsa_batch",
    )(page_table, *([ks_cache] * (group * n_pages)), *([vs_cache] * (group * n_pages)), kwin, vwin, kc, vc,
      proj, proj, proj, proj, proj, gate_rows)


S5_SEQ_CHUNK = 256
S5_BATCH_SEQS = 16


def kernel(x_prompt, x_sample, cache_k_cmp, cache_v_cmp, cache_k_sel, cache_v_sel, cache_k_win, cache_v_win, state_ssm_re, state_ssm_im, state_pool, page_table, w_in, ssm_a_re, ssm_a_im, ssm_log_dt, ssm_b_re, ssm_b_im, ssm_c_re, ssm_c_im, ssm_d, ssm_w_glu, pool_w, pool_scale, cmp_pe, cmp_w1, cmp_w2, w_out, ln1_g, ln1_b, mlp_w1, mlp_w2, ln2_g, ln2_b):
    bp, lp, d = x_prompt.shape
    nb, steps, _ = x_sample.shape
    assert bp == 1 and d == D_MODEL and steps < CMP_STRIDE
    n_p, n_s = bp * lp, nb * steps
    n_phys = cache_k_cmp.shape[1]
    past_len = page_table.shape[1] * PAGE_SIZE
    assert cache_k_win.shape[2] == WINDOW and past_len >= WINDOW

    x = jnp.concatenate([x_prompt.reshape(n_p, d), x_sample.reshape(n_s, d)], axis=0)
    rows_last = lambda c: c.transpose(0, 1, 3, 4, 2).reshape(c.shape[0], c.shape[1], KV_WIDTH, c.shape[2])
    kcmp_pages, vcmp_pages = rows_last(cache_k_cmp), rows_last(cache_v_cmp)
    ksel_pages, vsel_pages = rows_last(cache_k_sel), rows_last(cache_v_sel)
    kwin_t, vwin_t = rows_last(cache_k_win), rows_last(cache_v_win)
    kv_col0 = sum(PROJ_SIZES[:3])
    zero_state = jnp.zeros((1, SSM_FLAT), F32)
    zero_buf = jnp.zeros((POOL_HIST, POOL_WIDTH), F32)
    row = lambda v: v.reshape(1, -1)

    new_p, new_s = [], []
    for l in range(DEPTH):
        w_in_l = jnp.pad(w_in[l], ((0, 0), (0, PROJ_PAD - PROJ_WIDTH)))
        proj = _input_projection(x, w_in_l)
        kv = [proj[:, kv_col0 + j * KV_WIDTH:kv_col0 + (j + 1) * KV_WIDTH] for j in range(6)]
        gate_logits = proj[:, GATE_COL:GATE_COL + 3 * N_HEADS]

        s5w = _s5_weights(ssm_a_re[l], ssm_a_im[l], ssm_log_dt[l], ssm_b_re[l], ssm_b_im[l], ssm_c_re[l],
                          ssm_c_im[l], ssm_d[l], ssm_w_glu[l])
        ys_p, hr_p, hi_p = _s5_mixer(proj, 0, n_p, 1, S5_SEQ_CHUNK, True, s5w, zero_state, zero_state)
        ys_s, hr_s, hi_s = _s5_mixer(proj, n_p, n_s, S5_BATCH_SEQS, steps, False, s5w,
                                     state_ssm_re[l].reshape(nb, SSM_FLAT), state_ssm_im[l].reshape(nb, SSM_FLAT))

        pw, psc = pool_w[l].astype(BF16), row(pool_scale[l])
        yp_p, pool_p = _pool_mixer_seq(proj, 0, n_p, 0, zero_buf, pw, psc)
        u_t = proj[n_p:, SSM_WIDTH:SSM_WIDTH + POOL_WIDTH].reshape(nb, steps, POOL_WIDTH).transpose(1, 0, 2)
        yp_s_t, pool_s_t = _pool_mixer_batch(u_t, state_pool[l].transpose(1, 0, 2), past_len, pw, psc)
        yp_s = yp_s_t.transpose(1, 0, 2).reshape(n_s, POOL_WIDTH)

        cw_k = _compress_weights(cmp_pe[l, 0], cmp_w1[l, 0], cmp_w2[l, 0])
        cw_v = _compress_weights(cmp_pe[l, 1], cmp_w1[l, 1], cmp_w2[l, 1])
        kc_p = _compress_seq(proj, kv_col0 // KV_WIDTH, n_p, cw_k)
        vc_p = _compress_seq(proj, kv_col0 // KV_WIDTH + 1, n_p, cw_v)
        tiles = lambda a: a[:n_p].astype(BF16).reshape(n_p // KEY_STEP, KEY_STEP, KV_WIDTH)
        yn_p = _nsa_seq(proj, gate_logits[:n_p].T, kc_p, vc_p.T, tiles(kv[2]), _value_steps(kv[3][:n_p]),
                        tiles(kv[4]), _value_steps(kv[5][:n_p]), n_p)

        kc_s = _compress_paged(kcmp_pages, l, page_table, cw_k)
        vc_s = _compress_paged(vcmp_pages, l, page_table, cw_v)
        gate_rows = gate_logits[n_p:].reshape(nb, steps, N_HEADS, 3).transpose(0, 2, 1, 3)
        gate_rows = jnp.pad(gate_rows.reshape(nb, N_HEADS * steps, 3), ((0, 0), (0, 0), (0, SUBLANES - 3)))
        yn_s, kw_s, vw_s = _nsa_batch(proj, n_p, steps, past_len, gate_rows, kc_s, vc_s, ksel_pages, vsel_pages, l,
                                      page_table, kwin_t[l], vwin_t[l])

        y_ssm = jnp.concatenate([ys_p, ys_s], axis=0)
        y_pool = jnp.concatenate([yp_p, yp_s], axis=0)
        y_nsa = jnp.concatenate([yn_p, yn_s.astype(BF16)], axis=0)
        x = _output_projection_ln(x, y_ssm, y_pool, y_nsa, w_out[l], row(ln1_g[l]), row(ln1_b[l]))
        x = _mlp_ln(x, mlp_w1[l], mlp_w2[l], row(ln2_g[l]), row(ln2_b[l]))

        heads = lambda a, b_, t: a.reshape(b_, t, N_KV, HEAD_DIM)
        rows_first = lambda a: a.reshape(nb, N_KV, HEAD_DIM, WINDOW).transpose(0, 3, 1, 2)
        n_keep = min(WINDOW, lp)
        new_p.append([heads(kv[j][:n_p], bp, lp) for j in range(4)]
                     + [heads(kv[j][n_p - n_keep:n_p], bp, n_keep) for j in (4, 5)]
                     + [hr_p.reshape(bp, SSM_GROUPS, SSM_STATE), hi_p.reshape(bp, SSM_GROUPS, SSM_STATE),
                        pool_p[POOL_HIST - POOL_BUF:].reshape(bp, POOL_BUF, POOL_WIDTH)])
        new_s.append([heads(kv[j][n_p:], nb, steps) for j in range(4)]
                     + [rows_first(kw_s), rows_first(vw_s)]
                     + [hr_s.reshape(nb, SSM_GROUPS, SSM_STATE), hi_s.reshape(nb, SSM_GROUPS, SSM_STATE),
                        pool_s_t.transpose(1, 0, 2)])

    st_p = [jnp.stack(f) for f in zip(*new_p)]
    st_s = [jnp.stack(f) for f in zip(*new_s)]
    out = [x[:n_p].reshape(bp, lp, d), x[n_p:].reshape(nb, steps, d)]
    for a, b_ in zip(st_p, st_s):
        out += [a, b_]
    return tuple(out)
```

```python
import functools
import math

import jax
import jax.numpy as jnp
from jax import lax
from jax.experimental import pallas as pl
from jax.experimental.pallas import tpu as pltpu

F32 = jnp.float32
BF16 = jnp.bfloat16

D_MODEL = 2048
DEPTH = 2
PAGE_SIZE = 128
SSM_WIDTH = 512
SSM_GROUP_CH = 16
SSM_GROUPS = 32
SSM_STATE = 64
SSM_FLAT = SSM_GROUPS * SSM_STATE
POOL_WIDTH = 512
POOL_WINDOWS = (2, 4, 8, 16)
POOL_CH = 128
POOL_BUF = 15
NSA_WIDTH = 1024
HEAD_DIM = 64
N_HEADS = 16
N_KV = 4
GQA = 4
KV_WIDTH = N_KV * HEAD_DIM
CMP_STRIDE = 16
CMP_BLOCK = 32
CMP_HIDDEN = 128
SEL_BLOCK = 64
TOP_N = 16
WINDOW = 512
Q_BLOCK = 128
D_FF = 4 * D_MODEL
ALPHA = (2 * DEPTH) ** 0.25
LN_EPS = 1e-5
NEG_INF = -1e30
TINY = 1e-30
FORCED_SCORE = 1e4
PROJ_SIZES = (SSM_WIDTH, POOL_WIDTH, NSA_WIDTH) + (KV_WIDTH,) * 6 + (3 * N_HEADS,)
PROJ_WIDTH = sum(PROJ_SIZES)
PROJ_PAD = 3840
GATE_COL = 3584

LANES = 128
SUBLANES = 8
VMEM_LIMIT = 56 * 1024 * 1024


def _cparams(*sem):
    return pltpu.CompilerParams(dimension_semantics=sem, vmem_limit_bytes=VMEM_LIMIT)


def _gelu(x):
    return 0.5 * x * (1.0 + jnp.tanh(math.sqrt(2.0 / math.pi) * (x + 0.044715 * (x * x * x))))


def _sigmoid(x):
    return 1.0 / (1.0 + jnp.exp(-x))


def _layer_norm(z, g, b):
    zc = z - jnp.mean(z, axis=-1, keepdims=True)
    var = jnp.mean(zc * zc, axis=-1, keepdims=True)
    return zc * lax.rsqrt(var + LN_EPS) * g + b


def _proj_kernel(x_ref, w_ref, o_ref, ob_ref, xb_ref):
    @pl.when(pl.program_id(1) == 0)
    def _():
        xb_ref[...] = x_ref[...].astype(BF16)

    o = jnp.dot(xb_ref[...], w_ref[...].astype(BF16), preferred_element_type=F32)
    o_ref[...] = o
    ob_ref[...] = o.astype(BF16)


def _input_projection(x, w, layer, tm=1024, tn=768):
    m, k = x.shape
    n = w.shape[2]
    return pl.pallas_call(
        _proj_kernel,
        grid=(m // tm, n // tn),
        in_specs=[pl.BlockSpec((tm, k), lambda i, j: (i, 0)),
                  pl.BlockSpec((None, k, tn), lambda i, j: (layer, 0, j))],
        out_specs=[pl.BlockSpec((tm, tn), lambda i, j: (i, j)), pl.BlockSpec((tm, tn), lambda i, j: (i, j))],
        out_shape=[jax.ShapeDtypeStruct((m, n), F32), jax.ShapeDtypeStruct((m, n), BF16)],
        scratch_shapes=[pltpu.VMEM((tm, k), BF16)],
        compiler_params=_cparams("parallel", "arbitrary"),
        name="input_projection",
    )(x, w)


def _outproj_ln_kernel(x_ref, ysp_ref, yss_ref, ypp_ref, yps_ref, ynp_ref, yns_ref, w_ref, g_ref, b_ref, o_ref, wb_ref,
                       *, p_tiles):
    i = pl.program_id(0)

    @pl.when(i == 0)
    def _():
        wb_ref[...] = w_ref[...].astype(BF16)

    from_prompt = i < p_tiles
    ys = jnp.where(from_prompt, ysp_ref[...], yss_ref[...])
    yp = jnp.where(from_prompt, ypp_ref[...], yps_ref[...])
    yn = jnp.where(from_prompt, ynp_ref[...], yns_ref[...].astype(BF16))
    acc = jnp.dot(ys, wb_ref[0:SSM_WIDTH, :], preferred_element_type=F32)
    acc += jnp.dot(yp, wb_ref[SSM_WIDTH:SSM_WIDTH + POOL_WIDTH, :], preferred_element_type=F32)
    acc += jnp.dot(yn, wb_ref[SSM_WIDTH + POOL_WIDTH:, :], preferred_element_type=F32)
    o_ref[...] = _layer_norm(ALPHA * x_ref[...] + acc, g_ref[...], b_ref[...])


def _output_projection_ln(x, n_p, y_ssm, y_pool, y_nsa, w, layer, g, b, tm=512):
    m, d = x.shape
    p_tiles = n_p // tm
    row = lambda i: (i, 0)
    prow = lambda i: (jnp.minimum(i, p_tiles - 1), 0)
    srow = lambda i: (jnp.maximum(i - p_tiles, 0), 0)
    fixed = lambda i: (0, 0)
    pair = lambda width: [pl.BlockSpec((tm, width), prow), pl.BlockSpec((tm, width), srow)]
    return pl.pallas_call(
        functools.partial(_outproj_ln_kernel, p_tiles=p_tiles),
        grid=(m // tm,),
        in_specs=[pl.BlockSpec((tm, d), row)] + pair(SSM_WIDTH) + pair(POOL_WIDTH) + pair(NSA_WIDTH)
        + [pl.BlockSpec((None, d, d), lambda i: (layer, 0, 0), pipeline_mode=pl.Buffered(1)),
           pl.BlockSpec((1, d), fixed),
           pl.BlockSpec((1, d), fixed)],
        out_specs=pl.BlockSpec((tm, d), row),
        out_shape=jax.ShapeDtypeStruct((m, d), F32),
        scratch_shapes=[pltpu.VMEM((d, d), BF16)],
        compiler_params=_cparams("arbitrary"),
        name="output_projection_ln",
    )(x, *y_ssm, *y_pool, *y_nsa, w, g, b)


def _mlp_ln_kernel(x_ref, w1_ref, w2_ref, g_ref, b_ref, o_ref, xb_ref, acc_ref):
    f = pl.program_id(1)

    @pl.when(f == 0)
    def _():
        xb_ref[...] = x_ref[...].astype(BF16)
        acc_ref[...] = jnp.zeros_like(acc_ref)

    h = jnp.dot(xb_ref[...], w1_ref[...].astype(BF16), preferred_element_type=F32)
    h = jnp.square(jnp.maximum(h, 0.0)).astype(BF16)
    acc_ref[...] += jnp.dot(h, w2_ref[...].astype(BF16), preferred_element_type=F32)

    @pl.when(f == pl.num_programs(1) - 1)
    def _():
        o_ref[...] = _layer_norm(ALPHA * x_ref[...] + acc_ref[...], g_ref[...], b_ref[...])


def _mlp_ln(x, w1, w2, layer, g, b, tm=1024, tf=512):
    m, d = x.shape
    ff = w1.shape[2]
    once = pl.Buffered(1)
    return pl.pallas_call(
        _mlp_ln_kernel,
        grid=(m // tm, ff // tf),
        in_specs=[pl.BlockSpec((tm, d), lambda i, f: (i, 0), pipeline_mode=once),
                  pl.BlockSpec((None, d, tf), lambda i, f: (layer, 0, f)),
                  pl.BlockSpec((None, tf, d), lambda i, f: (layer, f, 0)),
                  pl.BlockSpec((1, d), lambda i, f: (0, 0)),
                  pl.BlockSpec((1, d), lambda i, f: (0, 0))],
        out_specs=pl.BlockSpec((tm, d), lambda i, f: (i, 0), pipeline_mode=once),
        out_shape=jax.ShapeDtypeStruct((m, d), F32),
        scratch_shapes=[pltpu.VMEM((tm, d), BF16), pltpu.VMEM((tm, d), F32)],
        compiler_params=_cparams("parallel", "arbitrary"),
        name="mlp_ln",
    )(x, w1, w2, g, b)


S5_GROUP_BLOCKS = 4
S5_BLOCK_LANES = SSM_FLAT // S5_GROUP_BLOCKS // LANES
S5_SLABS = SSM_FLAT // LANES


def _s5_kernel(u_ref, bre_ref, bim_ref, cre_ref, cim_ref, vec_ref, d_ref, wglu_ref, h0re_ref, h0im_ref,
               y_ref, hre_out, him_out, xre_ref, xim_ref, hre_s, him_s, *, n_seq, steps, carry):
    u = u_ref[...]
    ub = u.astype(BF16)
    cw = SSM_WIDTH // S5_GROUP_BLOCKS
    sw = SSM_FLAT // S5_GROUP_BLOCKS
    for j in range(S5_GROUP_BLOCKS):
        uj = ub[:, j * cw:(j + 1) * cw]
        bur = jnp.dot(uj, bre_ref[j], preferred_element_type=F32)
        bui = jnp.dot(uj, bim_ref[j], preferred_element_type=F32)
        zr = vec_ref[2:3, j * sw:(j + 1) * sw]
        zi = vec_ref[3:4, j * sw:(j + 1) * sw]
        xr = zr * bur - zi * bui
        xi = zr * bui + zi * bur
        for q in range(S5_BLOCK_LANES):
            xre_ref[j * S5_BLOCK_LANES + q] = xr[:, q * LANES:(q + 1) * LANES]
            xim_ref[j * S5_BLOCK_LANES + q] = xi[:, q * LANES:(q + 1) * LANES]

    if carry:
        @pl.when(pl.program_id(0) == 0)
        def _():
            hre_s[...] = h0re_ref[...]
            him_s[...] = h0im_ref[...]
    else:
        hre_s[...] = h0re_ref[...]
        him_s[...] = h0im_ref[...]

    for j in range(S5_GROUP_BLOCKS):
        slabs = [j * S5_BLOCK_LANES + q for q in range(S5_BLOCK_LANES)]
        lanes = [slice(s * LANES, (s + 1) * LANES) for s in slabs]
        ar = [jnp.broadcast_to(vec_ref[0:1, l], (n_seq, LANES)) for l in lanes]
        ai = [jnp.broadcast_to(vec_ref[1:2, l], (n_seq, LANES)) for l in lanes]

        def step(t, h, slabs=slabs, ar=ar, ai=ai):
            rows = pl.ds(t, n_seq, stride=steps) if n_seq > 1 else pl.ds(t, 1)
            out = []
            for q, s in enumerate(slabs):
                hr, hi = h[2 * q], h[2 * q + 1]
                nr = ar[q] * hr - ai[q] * hi + xre_ref[s, rows, :]
                ni = ar[q] * hi + ai[q] * hr + xim_ref[s, rows, :]
                xre_ref[s, rows, :] = nr
                xim_ref[s, rows, :] = ni
                out += [nr, ni]
            return tuple(out)

        h0 = []
        for l in lanes:
            h0 += [hre_s[:, l], him_s[:, l]]
        hT = lax.fori_loop(0, steps, step, tuple(h0), unroll=8)
        for q, l in enumerate(lanes):
            hre_s[:, l] = hT[2 * q]
            him_s[:, l] = hT[2 * q + 1]

    hre_out[...] = hre_s[...]
    him_out[...] = him_s[...]

    ys = []
    for j in range(S5_GROUP_BLOCKS):
        slabs = range(j * S5_BLOCK_LANES, (j + 1) * S5_BLOCK_LANES)
        hr = jnp.concatenate([xre_ref[s] for s in slabs], axis=-1).astype(BF16)
        hi = jnp.concatenate([xim_ref[s] for s in slabs], axis=-1).astype(BF16)
        ys.append(jnp.dot(hr, cre_ref[j], preferred_element_type=F32)
                  - jnp.dot(hi, cim_ref[j], preferred_element_type=F32))
    y = jnp.concatenate(ys, axis=-1) + d_ref[...] * u
    z = _gelu(y)
    gate = _sigmoid(jnp.dot(z.astype(BF16), wglu_ref[...], preferred_element_type=F32))
    y_ref[...] = (z * gate).astype(BF16)


def _s5_mixer(proj, row0, n_rows, n_seq, steps, carry, wts, h0_re, h0_im):
    bre, bim, cre, cim, vec, d, wglu = wts
    chunk = n_seq * steps
    n_chunks = n_rows // chunk
    blk0 = row0 // chunk
    st_rows = h0_re.shape[0]
    st_map = (lambda i: (0, 0)) if carry else (lambda i: (i, 0))
    fixed2 = lambda i: (0, 0)
    fixed3 = lambda i: (0, 0, 0)
    kern = functools.partial(_s5_kernel, n_seq=n_seq, steps=steps, carry=carry)
    return pl.pallas_call(
        kern,
        grid=(n_chunks,),
        in_specs=[pl.BlockSpec((chunk, SSM_WIDTH), lambda i: (blk0 + i, 0)),
                  pl.BlockSpec(bre.shape, fixed3), pl.BlockSpec(bim.shape, fixed3),
                  pl.BlockSpec(cre.shape, fixed3), pl.BlockSpec(cim.shape, fixed3),
                  pl.BlockSpec(vec.shape, fixed2), pl.BlockSpec(d.shape, fixed2),
                  pl.BlockSpec(wglu.shape, fixed2),
                  pl.BlockSpec((n_seq, SSM_FLAT), st_map), pl.BlockSpec((n_seq, SSM_FLAT), st_map)],
        out_specs=[pl.BlockSpec((chunk, SSM_WIDTH), lambda i: (i, 0)),
                   pl.BlockSpec((n_seq, SSM_FLAT), st_map), pl.BlockSpec((n_seq, SSM_FLAT), st_map)],
        out_shape=[jax.ShapeDtypeStruct((n_rows, SSM_WIDTH), BF16),
                   jax.ShapeDtypeStruct((st_rows, SSM_FLAT), F32),
                   jax.ShapeDtypeStruct((st_rows, SSM_FLAT), F32)],
        scratch_shapes=[pltpu.VMEM((S5_SLABS, chunk, LANES), F32), pltpu.VMEM((S5_SLABS, chunk, LANES), F32),
                        pltpu.VMEM((n_seq, SSM_FLAT), F32), pltpu.VMEM((n_seq, SSM_FLAT), F32)],
        compiler_params=_cparams("arbitrary"),
        name="s5_mixer_carry" if carry else "s5_mixer_batch",
    )(proj, bre, bim, cre, cim, vec, d, wglu, h0_re, h0_im)


def _s5_weights(a_re, a_im, log_dt, b_re, b_im, c_re, c_im, d, w_glu):
    dt = jnp.exp(log_dt)[:, None]
    mag = jnp.exp(a_re * dt)
    abar_re, abar_im = mag * jnp.cos(a_im * dt), mag * jnp.sin(a_im * dt)
    den = a_re * a_re + a_im * a_im
    zr = ((abar_re - 1.0) * a_re + abar_im * a_im) / den
    zi = (abar_im * a_re - (abar_re - 1.0) * a_im) / den
    flat = lambda v: v.reshape(1, SSM_FLAT)
    vec = jnp.concatenate([flat(abar_re), flat(abar_im), flat(zr), flat(zi),
                           jnp.zeros((SUBLANES - 4, SSM_FLAT), F32)], axis=0)
    gb = SSM_GROUPS // S5_GROUP_BLOCKS
    eye = jnp.eye(gb, dtype=F32)

    def pack_b(b):
        bb = b.reshape(S5_GROUP_BLOCKS, gb, SSM_STATE, SSM_GROUP_CH)
        m = jnp.einsum('jgph,gk->jghkp', bb, eye)
        return m.reshape(S5_GROUP_BLOCKS, gb * SSM_GROUP_CH, gb * SSM_STATE).astype(BF16)

    def pack_c(c):
        cc = c.reshape(S5_GROUP_BLOCKS, gb, SSM_GROUP_CH, SSM_STATE)
        m = jnp.einsum('jghp,gk->jgpkh', cc, eye)
        return m.reshape(S5_GROUP_BLOCKS, gb * SSM_STATE, gb * SSM_GROUP_CH).astype(BF16)

    return (pack_b(b_re), pack_b(b_im), pack_c(c_re), pack_c(c_im), vec, d.reshape(1, SSM_WIDTH),
            w_glu.astype(BF16))


POOL_HIST = 16


def _pool_seq_kernel(u_ref, buf_ref, w_ref, scale_ref, y_ref, new_ref, xc_ref, *, chunk, pos0):
    i = pl.program_id(0)

    @pl.when(i == 0)
    def _():
        xc_ref[0:POOL_HIST, :] = buf_ref[...]

    u = u_ref[...]
    xc_ref[POOL_HIST:POOL_HIST + chunk, :] = u
    pos = pos0 + i * chunk + lax.broadcasted_iota(jnp.int32, (chunk, POOL_CH), 0)
    outs = []
    for g, wd in enumerate(POOL_WINDOWS):
        lanes = slice(g * POOL_CH, (g + 1) * POOL_CH)
        ug = u[:, lanes]
        acc = ug
        for k in range(1, wd):
            acc = acc + xc_ref[pl.ds(POOL_HIST - k, chunk), lanes]
        mix = acc / jnp.minimum(wd, pos + 1).astype(F32) - ug
        outs.append(jnp.dot(mix.astype(BF16), w_ref[g], preferred_element_type=F32))
    y_ref[...] = (jnp.concatenate(outs, axis=-1) * scale_ref[...]).astype(BF16)
    tail = xc_ref[chunk:chunk + POOL_HIST, :]
    xc_ref[0:POOL_HIST, :] = tail
    new_ref[...] = tail


def _pool_mixer_seq(proj, row0, n_rows, pos0, buf16, w, scale, chunk=256):
    blk0 = row0 // chunk
    kern = functools.partial(_pool_seq_kernel, chunk=chunk, pos0=pos0)
    return pl.pallas_call(
        kern,
        grid=(n_rows // chunk,),
        in_specs=[pl.BlockSpec((chunk, POOL_WIDTH), lambda i: (blk0 + i, 1)),
                  pl.BlockSpec((POOL_HIST, POOL_WIDTH), lambda i: (0, 0)),
                  pl.BlockSpec(w.shape, lambda i: (0, 0, 0)),
                  pl.BlockSpec((1, POOL_WIDTH), lambda i: (0, 0))],
        out_specs=[pl.BlockSpec((chunk, POOL_WIDTH), lambda i: (i, 0)),
                   pl.BlockSpec((POOL_HIST, POOL_WIDTH), lambda i: (0, 0))],
        out_shape=[jax.ShapeDtypeStruct((n_rows, POOL_WIDTH), BF16),
                   jax.ShapeDtypeStruct((POOL_HIST, POOL_WIDTH), F32)],
        scratch_shapes=[pltpu.VMEM((POOL_HIST + chunk, POOL_WIDTH), F32)],
        compiler_params=_cparams("arbitrary"),
        name="pool_mixer_seq",
    )(proj, buf16, w, scale)


def _pool_batch_kernel(u_ref, buf_ref, w_ref, scale_ref, y_ref, new_ref, *, steps, pos0):
    def xrow(j, lanes):
        return buf_ref[j, :, lanes] if j < POOL_BUF else u_ref[j - POOL_BUF, :, lanes]

    nb = u_ref.shape[1]
    for g, wd in enumerate(POOL_WINDOWS):
        lanes = slice(g * POOL_CH, (g + 1) * POOL_CH)
        mixes = []
        for t in range(steps):
            ug = u_ref[t, :, lanes]
            acc = ug
            for k in range(1, wd):
                acc = acc + xrow(POOL_BUF + t - k, lanes)
            mixes.append(acc / float(min(wd, pos0 + t + 1)) - ug)
        mix = jnp.concatenate(mixes, axis=0).astype(BF16)
        yg = jnp.dot(mix, w_ref[g], preferred_element_type=F32) * scale_ref[:, lanes]
        for t in range(steps):
            y_ref[t, :, lanes] = yg[t * nb:(t + 1) * nb].astype(BF16)
    for j in range(POOL_BUF):
        new_ref[j] = xrow(steps + j, slice(None))


def _pool_mixer_batch(u_t, buf_t, pos0, w, scale):
    steps, nb, _ = u_t.shape
    kern = functools.partial(_pool_batch_kernel, steps=steps, pos0=pos0)
    full3 = lambda i: (0, 0, 0)
    return pl.pallas_call(
        kern,
        grid=(1,),
        in_specs=[pl.BlockSpec(u_t.shape, full3), pl.BlockSpec(buf_t.shape, full3),
                  pl.BlockSpec(w.shape, full3), pl.BlockSpec((1, POOL_WIDTH), lambda i: (0, 0))],
        out_specs=[pl.BlockSpec(u_t.shape, full3), pl.BlockSpec(buf_t.shape, full3)],
        out_shape=[jax.ShapeDtypeStruct(u_t.shape, BF16), jax.ShapeDtypeStruct(buf_t.shape, F32)],
        compiler_params=_cparams("arbitrary"),
        name="pool_mixer_batch",
    )(u_t, buf_t, w, scale)


CMP_HALVES = KV_WIDTH // LANES
CMP_PAIR = LANES // HEAD_DIM
CMP_K = CMP_STRIDE * LANES
CMP_N = 2 * CMP_PAIR * CMP_HIDDEN


def _compress_kernel(*refs, n_seq, n_blk, transposed):
    if transposed:
        refs = refs[1:]
    blk_refs = refs[:n_seq * n_blk]
    wpair_ref, pe_ref, w1_ref, w2_ref, out_ref, xs_ref, g_ref, b_ref = refs[n_seq * n_blk:]
    blk_rows = blk_refs[0].shape[1] if transposed else blk_refs[0].shape[0]
    n = n_blk * blk_rows // CMP_STRIDE
    total = n_seq * n
    for i, r in enumerate(blk_refs):
        x = r[...].T if transposed else r[...]
        for h in range(CMP_HALVES):
            xs_ref[h, i * blk_rows:(i + 1) * blk_rows, :] = x[:, h * LANES:(h + 1) * LANES]
    ab = []
    for h in range(CMP_HALVES):
        for s_ in range(CMP_STRIDE):
            g_ref[h, :, s_ * LANES:(s_ + 1) * LANES] = (
                xs_ref[h, pl.ds(s_, total, stride=CMP_STRIDE), :].astype(BF16))
        ab.append(jnp.dot(g_ref[h], wpair_ref[...], preferred_element_type=F32))
    hid0 = jnp.dot(pe_ref[...], w1_ref[...], preferred_element_type=F32)[0:1]
    hid0 = jnp.concatenate([hid0] * CMP_PAIR, axis=-1)
    half_n = CMP_N // 2
    row = lax.broadcasted_iota(jnp.int32, (n, LANES), 0)
    b_ref[n:n + SUBLANES, :] = jnp.zeros((SUBLANES, half_n), F32)
    for h in range(CMP_HALVES):
        for q in range(n_seq):
            b_ref[0:n, :] = ab[h][q * n:(q + 1) * n, half_n:]
            hid = ab[h][q * n:(q + 1) * n, :half_n] + b_ref[pl.ds(1, n), :] + hid0
            out = jnp.dot(_gelu(hid).astype(BF16), w2_ref[...], preferred_element_type=F32)
            out_ref[q, :, h * LANES:(h + 1) * LANES] = jnp.where(row < n - 1, out, 0.0).astype(out_ref.dtype)


def _compress_weights(pe, w1, w2):
    w1b = w1.reshape(2, CMP_STRIDE, HEAD_DIM, CMP_HIDDEN)
    eye = jnp.eye(CMP_PAIR, dtype=F32)
    wpair = jnp.einsum('asdh,kj->skdajh', w1b, eye).reshape(CMP_K, CMP_N).astype(BF16)
    w2pair = jnp.einsum('hd,kj->khjd', w2, eye).reshape(CMP_PAIR * CMP_HIDDEN, LANES).astype(BF16)
    pe8 = jnp.concatenate([pe.reshape(1, CMP_BLOCK * HEAD_DIM),
                           jnp.zeros((SUBLANES - 1, CMP_BLOCK * HEAD_DIM), F32)], axis=0).astype(BF16)
    return wpair, pe8, w1.astype(BF16), w2pair


def _compress_scratch(n_seq, rows, n):
    return [pltpu.VMEM((CMP_HALVES, n_seq * rows, LANES), F32),
            pltpu.VMEM((CMP_HALVES, n_seq * n, CMP_K), BF16),
            pltpu.VMEM((n + SUBLANES, CMP_N // 2), F32)]


def _compress_seq(proj, col_block, n_rows, wts):
    n = n_rows // CMP_STRIDE
    full = lambda i: (0, 0)
    return pl.pallas_call(
        functools.partial(_compress_kernel, n_seq=1, n_blk=1, transposed=False),
        grid=(1,),
        in_specs=[pl.BlockSpec((n_rows, KV_WIDTH), lambda i: (0, col_block))]
        + [pl.BlockSpec(w.shape, full) for w in wts],
        out_specs=pl.BlockSpec((1, n, KV_WIDTH), lambda i: (0, 0, 0)),
        out_shape=jax.ShapeDtypeStruct((1, n, KV_WIDTH), BF16),
        scratch_shapes=_compress_scratch(1, n_rows, n),
        compiler_params=_cparams("arbitrary"),
        name="compress_seq",
    )(proj, *wts)[0]


def _compress_paged(cache_t, layer, page_table, wts, group=2):
    nb, n_pages = page_table.shape
    n = n_pages * PAGE_SIZE // CMP_STRIDE
    full = lambda b, pt: (0, 0)

    def page_spec(bl, p):
        return pl.BlockSpec((None, None, KV_WIDTH, PAGE_SIZE), lambda b, pt: (layer, pt[group * b + bl, p], 0, 0))

    grid_spec = pltpu.PrefetchScalarGridSpec(
        num_scalar_prefetch=1,
        grid=(nb // group,),
        in_specs=[page_spec(bl, p) for bl in range(group) for p in range(n_pages)]
        + [pl.BlockSpec(w.shape, full) for w in wts],
        out_specs=pl.BlockSpec((group, n, KV_WIDTH), lambda b, pt: (b, 0, 0)),
        scratch_shapes=_compress_scratch(group, n_pages * PAGE_SIZE, n),
    )
    return pl.pallas_call(
        functools.partial(_compress_kernel, n_seq=group, n_blk=n_pages, transposed=True),
        grid_spec=grid_spec,
        out_shape=jax.ShapeDtypeStruct((nb, n, KV_WIDTH), BF16),
        compiler_params=_cparams("arbitrary"),
        name="compress_paged",
    )(page_table, *([cache_t] * (group * n_pages)), *wts)


def _alibi_slope(h):
    return 2.0 ** (-8.0 * (h + 1) / N_HEADS)


NT_DIMS = (((1,), (1,)), ((), ()))
QROWS = GQA * Q_BLOCK
KEY_TILE = 128
WIN_TILES = WINDOW // KEY_TILE
KEY_STEP = 2 * KEY_TILE
T_FULL, T_DIAG, T_OLD, T_NONE = 0, 1, 2, 3
LOG2E = math.log2(math.e)
V_ROWS = HEAD_DIM + 16
CMP_TILES_PER_SLAB = LANES * CMP_STRIDE // Q_BLOCK


def _nsa_seq_kernel(q_ref, gate_ref, kc_ref, vct_ref, ks_ref, vst_ref, kw_ref, vwt_ref, o_ref,
                    qx_ref, bias_ref, sbuf_ref, m_ref, acc_ref, oc_ref, os_ref, ow_ref, *, n_cmp_pad, n_sel, n_steps):
    i = pl.program_id(0)
    base = i * Q_BLOCK

    lane = lax.broadcasted_iota(jnp.int32, (1, QROWS), 1)
    qq_i = lane % Q_BLOCK
    qq = qq_i.astype(F32)
    g_lane = lane // Q_BLOCK

    def slope_row(k):
        r = jnp.full((1, QROWS), LOG2E * _alibi_slope(GQA * k + GQA - 1), F32)
        for g in range(GQA - 1):
            r = jnp.where(g_lane == g, LOG2E * _alibi_slope(GQA * k + g), r)
        return r

    slopes = [slope_row(k) for k in range(N_KV)]

    @pl.when(i == 0)
    def _():
        kk = lax.broadcasted_iota(jnp.int32, (KEY_TILE, QROWS), 0).astype(F32)
        for half in range(2):
            for k in range(N_KV):
                b = -slopes[k] * (qq - (kk + float(half * KEY_TILE)))
                bias_ref[half, T_FULL, k] = b
                bias_ref[half, T_DIAG, k] = jnp.where(kk <= qq, b, NEG_INF)
                bias_ref[half, T_OLD, k] = jnp.where(kk > qq, b, NEG_INF)
                bias_ref[half, T_NONE, k] = jnp.full((KEY_TILE, QROWS), NEG_INF, F32)

    qs = q_ref[...] * (LOG2E * HEAD_DIM ** -0.5)
    low_half = lax.broadcasted_iota(jnp.int32, (Q_BLOCK, LANES), 1) < HEAD_DIM
    for k in range(N_KV):
        keep = low_half if k % 2 == 0 else jnp.logical_not(low_half)
        parts = []
        for g in range(GQA):
            h = GQA * k + g
            pair = qs[:, (h // 2) * LANES:(h // 2 + 1) * LANES]
            if h % 2 != k % 2:
                pair = pltpu.roll(pair, HEAD_DIM, 1)
            parts.append(jnp.where(keep, pair, 0.0).astype(BF16))
        qx_ref[k, :, 0:LANES] = jnp.concatenate(parts, axis=0)

    sjf = lax.broadcasted_iota(jnp.int32, (n_sel, Q_BLOCK), 0).astype(F32)
    curf = ((base + lax.broadcasted_iota(jnp.int32, (1, Q_BLOCK), 1)) // SEL_BLOCK).astype(F32)
    visible = sjf <= curf
    forced = visible & ((sjf == 0.0) | (sjf == curf) | (sjf == curf - 1.0))

    def compressed_and_select(n_vis):
        n_io = lax.broadcasted_iota(jnp.int32, (n_vis, QROWS), 0)
        distc = (base.astype(F32) + qq) - (n_io.astype(F32) * CMP_STRIDE + (CMP_BLOCK - 1) / 2.0)
        visc = (n_io * CMP_STRIDE + (CMP_BLOCK - 1)) <= (base + qq_i)
        sj_o = lax.broadcasted_iota(jnp.int32, (n_sel, n_vis), 0)
        nn_o = lax.broadcasted_iota(jnp.int32, (n_sel, n_vis), 1)
        ovt = jnp.where((nn_o * CMP_STRIDE < (sj_o + 1) * SEL_BLOCK)
                        & (nn_o * CMP_STRIDE + (CMP_BLOCK - 1) >= sj_o * SEL_BLOCK), 1.0, 0.0).astype(BF16)
        for k in range(N_KV):
            kl = slice((k // 2) * LANES, (k // 2 + 1) * LANES)
            vr = slice(k * HEAD_DIM, (k + 1) * HEAD_DIM)
            s = lax.dot_general(kc_ref[0:n_vis, kl], qx_ref[k, :, 0:LANES], NT_DIMS, preferred_element_type=F32)
            s = jnp.where(visc, s - slopes[k] * distc, NEG_INF)
            m = jnp.maximum(jnp.max(s, axis=0, keepdims=True), 0.5 * NEG_INF)
            p = jnp.exp2(s - m)
            p = p * (1.0 / jnp.maximum(jnp.sum(p, axis=0, keepdims=True), TINY))
            oc_ref[k] = jnp.dot(vct_ref[vr, 0:n_vis], p.astype(BF16), preferred_element_type=F32)
            psum = p[:, 0:Q_BLOCK]
            for g in range(1, GQA):
                psum = psum + p[:, g * Q_BLOCK:(g + 1) * Q_BLOCK]
            p_hi = psum.astype(BF16)
            p_lo = (psum - p_hi.astype(F32)).astype(BF16)
            imp = (jnp.dot(ovt, p_hi, preferred_element_type=F32)
                   + jnp.dot(ovt, p_lo, preferred_element_type=F32))
            score = jnp.where(forced, FORCED_SCORE, jnp.where(visible, imp, -1.0))
            sel = jnp.zeros_like(score)
            for _ in range(min(TOP_N, n_sel)):
                mx = jnp.max(score, axis=0, keepdims=True)
                idx = jnp.min(jnp.where(score == mx, sjf, 1e9), axis=0, keepdims=True)
                hit = sjf == idx
                sel = jnp.where(hit & (mx >= 0.0), 1.0, sel)
                score = jnp.where(hit, -2.0, score)
            selq = jnp.where(sel > 0.0, 0.0, NEG_INF).T.astype(BF16)
            if n_sel < LANES:
                selq = jnp.concatenate([selq, jnp.zeros((Q_BLOCK, LANES - n_sel), BF16)], axis=1)
            qx_ref[k, :, LANES:2 * LANES] = jnp.concatenate([selq] * GQA, axis=0)

    n_slabs = -(-n_cmp_pad // LANES)
    for v in range(n_slabs):
        pl.when(i // CMP_TILES_PER_SLAB == v)(
            functools.partial(compressed_and_select, min((v + 1) * LANES, n_cmp_pad)))

    blk_of_key = lax.broadcasted_iota(jnp.int32, (KEY_STEP, LANES), 0) // SEL_BLOCK
    lane_id = lax.broadcasted_iota(jnp.int32, (KEY_STEP, LANES), 1)

    def flash(k_ref, vt_ref, j_lo, use_sel, out_ref):
        m_ref[...] = jnp.full(m_ref.shape, NEG_INF, F32)
        acc_ref[...] = jnp.zeros(acc_ref.shape, F32)

        def tile_type(t):
            ty = jnp.where(t == i, T_DIAG, jnp.where(t > i, T_NONE, T_FULL))
            if not use_sel:
                ty = jnp.where(t == i - WIN_TILES, T_OLD, jnp.where(t < i - WIN_TILES, T_NONE, ty))
            return ty

        def scores(j, heads):
            jj = jnp.minimum(j, n_steps - 1)
            ty0, ty1 = tile_type(2 * jj), tile_type(2 * jj + 1)
            if use_sel:
                onehot = jnp.where(lane_id == blk_of_key + (KEY_STEP // SEL_BLOCK) * jj, 1.0, 0.0).astype(BF16)
            out = []
            for k in heads:
                kl = slice((k // 2) * LANES, (k // 2 + 1) * LANES)
                if use_sel:
                    kx = jnp.concatenate([k_ref[jj, :, kl], onehot], axis=1)
                    s = lax.dot_general(kx, qx_ref[k], NT_DIMS, preferred_element_type=F32)
                else:
                    s = lax.dot_general(k_ref[jj, :, kl], qx_ref[k, :, 0:LANES], NT_DIMS,
                                        preferred_element_type=F32)
                out.append(s + jnp.concatenate([bias_ref[0, ty0, k], bias_ref[1, ty1, k]], axis=0))
            return out

        def softmax_pv(j, k, s):
            c = slopes[k] * (i * KEY_TILE - j * KEY_STEP).astype(F32)
            m_old = m_ref[k] + c
            m_new = jnp.maximum(m_old, jnp.max(s, axis=0, keepdims=True))
            alpha = jnp.exp2(m_old - m_new)
            p = jnp.exp2(s - m_new).astype(BF16)
            m_ref[k] = m_new - c
            acc_ref[k] = alpha * acc_ref[k] + jnp.dot(vt_ref[j, k * V_ROWS:(k + 1) * V_ROWS, :], p,
                                                      preferred_element_type=F32)

        sbuf_ref[0], sbuf_ref[1] = scores(j_lo, (0, 1))

        def body(j, carry):
            (s2,) = scores(j, (2,))
            softmax_pv(j, 0, sbuf_ref[0])
            (s3,) = scores(j, (3,))
            softmax_pv(j, 1, sbuf_ref[1])
            n0, n1 = scores(j + 1, (0, 1))
            sbuf_ref[0] = n0
            softmax_pv(j, 2, s2)
            sbuf_ref[1] = n1
            softmax_pv(j, 3, s3)
            return carry

        lax.fori_loop(j_lo, i // 2 + 1, body, 0)
        for k in range(N_KV):
            out_ref[k] = acc_ref[k, 0:HEAD_DIM] * (1.0 / jnp.maximum(acc_ref[k, HEAD_DIM:HEAD_DIM + 1], TINY))

    flash(ks_ref, vst_ref, 0, True, os_ref)
    flash(kw_ref, vwt_ref, jnp.maximum(i - WIN_TILES, 0) // 2, False, ow_ref)

    gate = _sigmoid(gate_ref[...])
    blocks = []
    for k in range(N_KV):
        for g in range(GQA):
            h = GQA * k + g
            sl = slice(g * Q_BLOCK, (g + 1) * Q_BLOCK)
            blocks.append(oc_ref[k, :, sl] * gate[3 * h:3 * h + 1]
                          + os_ref[k, :, sl] * gate[3 * h + 1:3 * h + 2]
                          + ow_ref[k, :, sl] * gate[3 * h + 2:3 * h + 3])
    o_ref[...] = jnp.concatenate(blocks, axis=0).T.astype(BF16)


def _value_steps(v):
    steps = v.shape[0] // KEY_STEP
    vt = v.reshape(steps, KEY_STEP, N_KV, HEAD_DIM).transpose(0, 2, 3, 1)
    ones = jnp.ones((steps, N_KV, 1, KEY_STEP), BF16)
    pad = jnp.zeros((steps, N_KV, V_ROWS - HEAD_DIM - 1, KEY_STEP), BF16)
    return jnp.concatenate([vt, ones, pad], axis=2).reshape(steps, N_KV * V_ROWS, KEY_STEP)


def _nsa_seq(proj, gate_t, kc, vct, ks_t, vst_t, kw_t, vwt_t, n_rows):
    n_tiles = n_rows // Q_BLOCK
    assert n_tiles % 2 == 0
    n_cmp_pad = kc.shape[0]
    n_sel = n_rows // SEL_BLOCK
    assert n_sel <= LANES
    kern = functools.partial(_nsa_seq_kernel, n_cmp_pad=n_cmp_pad, n_sel=n_sel, n_steps=n_tiles // 2)
    c2 = lambda i: (0, 0)
    c3 = lambda i: (0, 0, 0)
    st = (N_KV, HEAD_DIM, QROWS)
    return pl.pallas_call(
        kern,
        grid=(n_tiles,),
        in_specs=[pl.BlockSpec((Q_BLOCK, NSA_WIDTH), lambda i: (i, 1)),
                  pl.BlockSpec((3 * N_HEADS, Q_BLOCK), lambda i: (0, i)),
                  pl.BlockSpec(kc.shape, c2), pl.BlockSpec(vct.shape, c2),
                  pl.BlockSpec(ks_t.shape, c3), pl.BlockSpec(vst_t.shape, c3),
                  pl.BlockSpec(kw_t.shape, c3), pl.BlockSpec(vwt_t.shape, c3)],
        out_specs=pl.BlockSpec((Q_BLOCK, NSA_WIDTH), lambda i: (i, 0)),
        out_shape=jax.ShapeDtypeStruct((n_rows, NSA_WIDTH), BF16),
        scratch_shapes=[pltpu.VMEM((N_KV, QROWS, 2 * LANES), BF16),
                        pltpu.VMEM((2, 4, N_KV, KEY_TILE, QROWS), F32),
                        pltpu.VMEM((2, KEY_STEP, QROWS), F32),
                        pltpu.VMEM((N_KV, 1, QROWS), F32), pltpu.VMEM((N_KV, V_ROWS, QROWS), F32),
                        pltpu.VMEM(st, F32), pltpu.VMEM(st, F32), pltpu.VMEM(st, F32)],
        compiler_params=_cparams("arbitrary"),
        name="nsa_seq",
    )(proj, gate_t, kc, vct, ks_t, vst_t, kw_t, vwt_t)


def _nsa_batch_kernel(pt_ref, *refs, n_pages, steps, pos0, group):
    del pt_ref
    (kwin_ref, vwin_ref, kc_ref, vc_ref, q_ref, ksn_ref, vsn_ref, kwn_ref, vwn_ref, gate_ref,
     o_ref, kwo_ref, vwo_ref, bsel_ref, bwin_ref, bcmp_ref, e_ref, ov_ref) = refs[2 * group * n_pages:]
    rows = N_HEADS * steps
    past = n_pages * PAGE_SIZE
    nk = past + KEY_TILE
    nw = WINDOW + KEY_TILE

    @pl.when(pl.program_id(0) == 0)
    def _():
        def tables(width):
            r = lax.broadcasted_iota(jnp.int32, (rows, width), 0)
            c = lax.broadcasted_iota(jnp.int32, (rows, width), 1)
            slope = jnp.exp((-8.0 * math.log(2.0) / N_HEADS) * (r // steps + 1).astype(F32))
            return slope, pos0 + r % steps, c

        slope, qpos, key = tables(nk)
        bsel_ref[...] = jnp.where(key <= qpos, -slope * (qpos - key).astype(F32), NEG_INF)
        slope, qpos, w = tables(nw)
        kpos = pos0 - WINDOW + w
        d = qpos - kpos
        bwin_ref[...] = jnp.where((d >= 0) & (d < WINDOW) & (kpos >= 0), -slope * d.astype(F32), NEG_INF)
        slope, qpos, n = tables(LANES)
        c_mid = n.astype(F32) * CMP_STRIDE + (CMP_BLOCK - 1) / 2.0
        bcmp_ref[...] = jnp.where(n * CMP_STRIDE + (CMP_BLOCK - 1) <= qpos,
                                  -slope * (qpos.astype(F32) - c_mid), NEG_INF)
        sj = lax.broadcasted_iota(jnp.int32, (LANES, nk), 0)
        key = lax.broadcasted_iota(jnp.int32, (LANES, nk), 1)
        e_ref[...] = jnp.where(key // SEL_BLOCK == sj, 1.0, 0.0).astype(BF16)
        n = lax.broadcasted_iota(jnp.int32, (LANES, LANES), 0)
        sj = lax.broadcasted_iota(jnp.int32, (LANES, LANES), 1)
        ov_ref[...] = jnp.where((n * CMP_STRIDE < (sj + 1) * SEL_BLOCK)
                                & (n * CMP_STRIDE + (CMP_BLOCK - 1) >= sj * SEL_BLOCK), 1.0, 0.0).astype(BF16)

    def _nsa_batch_one(bl):
        ks_pages = refs[bl * n_pages:(bl + 1) * n_pages]
        vs_pages = refs[(group + bl) * n_pages:(group + bl + 1) * n_pages]
        tok = slice(bl * steps, (bl + 1) * steps)
        qs = q_ref[tok, :] * (HEAD_DIM ** -0.5)
        low_half = lax.broadcasted_iota(jnp.int32, (steps, LANES), 1) < HEAD_DIM
        zero = jnp.zeros((steps, LANES), F32)
        pieces = []
        for k in range(N_KV):
            keep = low_half if k % 2 == 0 else jnp.logical_not(low_half)
            for g in range(GQA):
                h = GQA * k + g
                pair = qs[:, (h // 2) * LANES:(h // 2 + 1) * LANES]
                if h % 2 != k % 2:
                    pair = pltpu.roll(pair, HEAD_DIM, 1)
                blk = jnp.where(keep, pair, 0.0)
                pieces.append(jnp.concatenate([blk, zero] if k // 2 == 0 else [zero, blk], axis=1))
        qb = jnp.concatenate(pieces, axis=0).astype(BF16)

        def new_tile(ref):
            return jnp.concatenate([ref[tok, :], jnp.zeros((KEY_TILE - steps, KV_WIDTH), F32)], axis=0).astype(BF16)

        def softmax_pv(s, v_parts, transposed):
            m = jnp.max(s, axis=-1, keepdims=True)
            p = jnp.exp(s - m)
            l = jnp.sum(p, axis=-1, keepdims=True)
            acc = None
            off = 0
            for v, v_t in zip(v_parts, transposed):
                n = v.shape[1] if v_t else v.shape[0]
                pb = p[:, off:off + n].astype(BF16)
                part = (lax.dot_general(pb, v, NT_DIMS, preferred_element_type=F32) if v_t
                        else jnp.dot(pb, v, preferred_element_type=F32))
                acc = part if acc is None else acc + part
                off += n
            return acc * (1.0 / jnp.maximum(l, TINY))

        bc = bcmp_ref[...]
        s_c = lax.dot_general(qb, kc_ref[bl], NT_DIMS, preferred_element_type=F32) + bc
        kt_all = jnp.concatenate([r[...] for r in ks_pages], axis=1).astype(BF16)
        s_s = jnp.concatenate([jnp.dot(qb, kt_all, preferred_element_type=F32),
                               lax.dot_general(qb, new_tile(ksn_ref), NT_DIMS, preferred_element_type=F32)],
                              axis=1) + bsel_ref[...]
        s_w = jnp.concatenate([jnp.dot(qb, kwin_ref[bl].astype(BF16), preferred_element_type=F32),
                               lax.dot_general(qb, new_tile(kwn_ref), NT_DIMS, preferred_element_type=F32)],
                              axis=1) + bwin_ref[...]
        yield

        m = jnp.max(s_c, axis=-1, keepdims=True)
        p = jnp.exp(s_c - m) * jnp.where(bc > 0.5 * NEG_INF, 1.0, 0.0)
        p = p * (1.0 / jnp.maximum(jnp.sum(p, axis=-1, keepdims=True), TINY))
        o_c = jnp.dot(p.astype(BF16), vc_ref[bl], preferred_element_type=F32)
        grp = GQA * steps
        psum = []
        for k in range(N_KV):
            acc = p[k * grp:k * grp + steps]
            for g in range(1, GQA):
                acc = acc + p[k * grp + g * steps:k * grp + (g + 1) * steps]
            psum.append(acc)
        psum = jnp.concatenate(psum, axis=0)
        p_hi = psum.astype(BF16)
        p_lo = (psum - p_hi.astype(F32)).astype(BF16)
        imp = (jnp.dot(p_hi, ov_ref[...], preferred_element_type=F32)
               + jnp.dot(p_lo, ov_ref[...], preferred_element_type=F32))
        yield

        o_w = softmax_pv(s_w, [vwin_ref[bl].astype(BF16), new_tile(vwn_ref)], [True, False])
        lane_w = lax.broadcasted_iota(jnp.int32, (KV_WIDTH, WINDOW), 1)
        for src, nw_ref, dst in ((kwin_ref, kwn_ref, kwo_ref), (vwin_ref, vwn_ref, vwo_ref)):
            new_t = jnp.concatenate([nw_ref[tok, :], jnp.zeros((LANES - steps, KV_WIDTH), F32)], axis=0).T
            new_t = jnp.concatenate([new_t] * (WINDOW // LANES), axis=1)
            shifted = pltpu.roll(src[bl], WINDOW - steps, 1)
            dst[bl] = jnp.where(lane_w < WINDOW - steps, shifted, pltpu.roll(new_t, WINDOW - steps, 1))
        yield

        sjf = lax.broadcasted_iota(jnp.int32, imp.shape, 1).astype(F32)
        step_of_row = lax.broadcasted_iota(jnp.int32, imp.shape, 0) % steps
        curf = ((pos0 + step_of_row) // SEL_BLOCK).astype(F32)
        visible = sjf <= curf
        forced = visible & ((sjf == 0.0) | (sjf == curf) | (sjf == curf - 1.0))
        score = jnp.where(forced, FORCED_SCORE, jnp.where(visible, imp, -1.0))
        n_blocks = -(-(past + steps) // SEL_BLOCK)
        rank = jnp.zeros_like(score)
        for jp in range(n_blocks):
            other = jnp.broadcast_to(score[:, jp:jp + 1], score.shape)
            beats = (other > score) | ((other == score) & (sjf > float(jp)))
            rank = rank + jnp.where(beats, 1.0, 0.0)
        sel = jnp.where((rank < float(TOP_N)) & (score >= 0.0), 1.0, 0.0)
        selk = jnp.dot(sel.astype(BF16), e_ref[...], preferred_element_type=F32)
        yield

        selb = jnp.where(selk > 0.5, 0.0, NEG_INF)
        sb = jnp.concatenate([selb[k * steps:(k + 1) * steps] for k in range(N_KV) for _ in range(GQA)], axis=0)
        vt_all = jnp.concatenate([r[...] for r in vs_pages], axis=1).astype(BF16)
        o_s = softmax_pv(s_s + sb, [vt_all, new_tile(vsn_ref)], [True, False])
        yield

        gates = _sigmoid(gate_ref[bl])
        y = o_c * gates[:, 0:1] + o_s * gates[:, 1:2] + o_w * gates[:, 2:3]
        for h in range(N_HEADS):
            k = h // GQA
            o_ref[tok, h * HEAD_DIM:(h + 1) * HEAD_DIM] = y[h * steps:(h + 1) * steps, k * HEAD_DIM:(k + 1) * HEAD_DIM]

    pending = [_nsa_batch_one(bl) for bl in range(group)]
    while pending:
        pending = [g for g in pending if next(g, True) is None]


def _nsa_batch(proj, row0, steps, pos0, gate_rows, kc, vc, ks_cache, vs_cache, layer, page_table, kwin, vwin,
               group=2):
    nb, n_pages = page_table.shape
    rows = N_HEADS * steps
    nk = n_pages * PAGE_SIZE + KEY_TILE
    tok = group * steps
    blk0 = row0 // tok
    kern = functools.partial(_nsa_batch_kernel, n_pages=n_pages, steps=steps, pos0=pos0, group=group)

    def page_spec(bl, p):
        return pl.BlockSpec((None, None, KV_WIDTH, PAGE_SIZE), lambda b, pt: (layer, pt[group * b + bl, p], 0, 0))

    pages = [page_spec(bl, p) for bl in range(group) for p in range(n_pages)]

    per_b = lambda b, pt: (b, 0, 0)
    kv_col0 = sum(PROJ_SIZES[:3]) // KV_WIDTH

    def new_spec(j):
        return pl.BlockSpec((tok, KV_WIDTH), lambda b, pt: (blk0 + b, kv_col0 + j))

    grid_spec = pltpu.PrefetchScalarGridSpec(
        num_scalar_prefetch=1,
        grid=(nb // group,),
        in_specs=pages + pages
        + [pl.BlockSpec((None, group, KV_WIDTH, WINDOW), lambda b, pt: (layer, b, 0, 0)),
           pl.BlockSpec((None, group, KV_WIDTH, WINDOW), lambda b, pt: (layer, b, 0, 0)),
           pl.BlockSpec((group,) + kc.shape[1:], per_b), pl.BlockSpec((group,) + vc.shape[1:], per_b),
           pl.BlockSpec((tok, NSA_WIDTH), lambda b, pt: (blk0 + b, 1)),
           new_spec(2), new_spec(3), new_spec(4), new_spec(5),
           pl.BlockSpec((group, rows, SUBLANES), per_b)],
        out_specs=[pl.BlockSpec((tok, NSA_WIDTH), lambda b, pt: (b, 0)),
                   pl.BlockSpec((group, KV_WIDTH, WINDOW), per_b), pl.BlockSpec((group, KV_WIDTH, WINDOW), per_b)],
        scratch_shapes=[pltpu.VMEM((rows, nk), F32), pltpu.VMEM((rows, WINDOW + KEY_TILE), F32),
                        pltpu.VMEM((rows, LANES), F32), pltpu.VMEM((LANES, nk), BF16),
                        pltpu.VMEM((LANES, LANES), BF16)],
    )
    return pl.pallas_call(
        kern,
        grid_spec=grid_spec,
        out_shape=[jax.ShapeDtypeStruct((nb * steps, NSA_WIDTH), F32),
                   jax.ShapeDtypeStruct((nb, KV_WIDTH, WINDOW), F32),
                   jax.ShapeDtypeStruct((nb, KV_WIDTH, WINDOW), F32)],
        compiler_params=_cparams("arbitrary"),
        name="nsa_batch",
    )(page_table, *([ks_cache] * (group * n_pages)), *([vs_cache] * (group * n_pages)), kwin, vwin, kc, vc,
      proj, proj, proj, proj, proj, gate_rows)


S5_SEQ_CHUNK = 256
S5_BATCH_SEQS = 16


def kernel(x_prompt, x_sample, cache_k_cmp, cache_v_cmp, cache_k_sel, cache_v_sel, cache_k_win, cache_v_win, state_ssm_re, state_ssm_im, state_pool, page_table, w_in, ssm_a_re, ssm_a_im, ssm_log_dt, ssm_b_re, ssm_b_im, ssm_c_re, ssm_c_im, ssm_d, ssm_w_glu, pool_w, pool_scale, cmp_pe, cmp_w1, cmp_w2, w_out, ln1_g, ln1_b, mlp_w1, mlp_w2, ln2_g, ln2_b):
    bp, lp, d = x_prompt.shape
    nb, steps, _ = x_sample.shape
    assert bp == 1 and d == D_MODEL and steps < CMP_STRIDE
    n_p, n_s = bp * lp, nb * steps
    n_phys = cache_k_cmp.shape[1]
    past_len = page_table.shape[1] * PAGE_SIZE
    assert cache_k_win.shape[2] == WINDOW and past_len >= WINDOW

    x = jnp.concatenate([x_prompt.reshape(n_p, d), x_sample.reshape(n_s, d)], axis=0)
    rows_last = lambda c: c.transpose(0, 1, 3, 4, 2).reshape(c.shape[0], c.shape[1], KV_WIDTH, c.shape[2])
    kcmp_pages, vcmp_pages = rows_last(cache_k_cmp), rows_last(cache_v_cmp)
    ksel_pages, vsel_pages = rows_last(cache_k_sel), rows_last(cache_v_sel)
    kwin_t, vwin_t = rows_last(cache_k_win), rows_last(cache_v_win)
    kv_col0 = sum(PROJ_SIZES[:3])
    zero_state = jnp.zeros((1, SSM_FLAT), F32)
    zero_buf = jnp.zeros((POOL_HIST, POOL_WIDTH), F32)
    row = lambda v: v.reshape(1, -1)

    w_in_pad = jnp.pad(w_in, ((0, 0), (0, 0), (0, PROJ_PAD - PROJ_WIDTH)))

    new_p, new_s = [], []
    for l in range(DEPTH):
        proj, proj_b = _input_projection(x, w_in_pad, l)
        kv = [proj[:, kv_col0 + j * KV_WIDTH:kv_col0 + (j + 1) * KV_WIDTH] for j in range(6)]
        kv_b = [proj_b[:n_p, kv_col0 + j * KV_WIDTH:kv_col0 + (j + 1) * KV_WIDTH] for j in range(6)]
        gate_logits = proj[:, GATE_COL:GATE_COL + 3 * N_HEADS]

        s5w = _s5_weights(ssm_a_re[l], ssm_a_im[l], ssm_log_dt[l], ssm_b_re[l], ssm_b_im[l], ssm_c_re[l],
                          ssm_c_im[l], ssm_d[l], ssm_w_glu[l])
        ys_p, hr_p, hi_p = _s5_mixer(proj, 0, n_p, 1, S5_SEQ_CHUNK, True, s5w, zero_state, zero_state)
        ys_s, hr_s, hi_s = _s5_mixer(proj, n_p, n_s, S5_BATCH_SEQS, steps, False, s5w,
                                     state_ssm_re[l].reshape(nb, SSM_FLAT), state_ssm_im[l].reshape(nb, SSM_FLAT))

        pw, psc = pool_w[l].astype(BF16), row(pool_scale[l])
        yp_p, pool_p = _pool_mixer_seq(proj, 0, n_p, 0, zero_buf, pw, psc)
        u_t = proj[n_p:, SSM_WIDTH:SSM_WIDTH + POOL_WIDTH].reshape(nb, steps, POOL_WIDTH).transpose(1, 0, 2)
        yp_s_t, pool_s_t = _pool_mixer_batch(u_t, state_pool[l].transpose(1, 0, 2), past_len, pw, psc)
        yp_s = yp_s_t.transpose(1, 0, 2).reshape(n_s, POOL_WIDTH)

        cw_k = _compress_weights(cmp_pe[l, 0], cmp_w1[l, 0], cmp_w2[l, 0])
        cw_v = _compress_weights(cmp_pe[l, 1], cmp_w1[l, 1], cmp_w2[l, 1])
        kc_p = _compress_seq(proj, kv_col0 // KV_WIDTH, n_p, cw_k)
        vc_p = _compress_seq(proj, kv_col0 // KV_WIDTH + 1, n_p, cw_v)
        tiles = lambda a: a.reshape(n_p // KEY_STEP, KEY_STEP, KV_WIDTH)
        yn_p = _nsa_seq(proj, gate_logits[:n_p].T, kc_p, vc_p.T, tiles(kv_b[2]), _value_steps(kv_b[3]),
                        tiles(kv_b[4]), _value_steps(kv_b[5]), n_p)

        kc_s = _compress_paged(kcmp_pages, l, page_table, cw_k)
        vc_s = _compress_paged(vcmp_pages, l, page_table, cw_v)
        gate_rows = gate_logits[n_p:].reshape(nb, steps, N_HEADS, 3).transpose(0, 2, 1, 3)
        gate_rows = jnp.pad(gate_rows.reshape(nb, N_HEADS * steps, 3), ((0, 0), (0, 0), (0, SUBLANES - 3)))
        yn_s, kw_s, vw_s = _nsa_batch(proj, n_p, steps, past_len, gate_rows, kc_s, vc_s, ksel_pages, vsel_pages, l,
                                      page_table, kwin_t, vwin_t)

        x = _output_projection_ln(x, n_p, (ys_p, ys_s), (yp_p, yp_s), (yn_p, yn_s), w_out, l,
                                  row(ln1_g[l]), row(ln1_b[l]))
        x = _mlp_ln(x, mlp_w1, mlp_w2, l, row(ln2_g[l]), row(ln2_b[l]))

        heads = lambda a, b_, t: a.reshape(b_, t, N_KV, HEAD_DIM)
        rows_first = lambda a: a.reshape(nb, N_KV, HEAD_DIM, WINDOW).transpose(0, 3, 1, 2)
        n_keep = min(WINDOW, lp)
        new_p.append([heads(kv[j][:n_p], bp, lp) for j in range(4)]
                     + [heads(kv[j][n_p - n_keep:n_p], bp, n_keep) for j in (4, 5)]
                     + [hr_p.reshape(bp, SSM_GROUPS, SSM_STATE), hi_p.reshape(bp, SSM_GROUPS, SSM_STATE),
                        pool_p[POOL_HIST - POOL_BUF:].reshape(bp, POOL_BUF, POOL_WIDTH)])
        new_s.append([heads(kv[j][n_p:], nb, steps) for j in range(4)]
                     + [rows_first(kw_s), rows_first(vw_s)]
                     + [hr_s.reshape(nb, SSM_GROUPS, SSM_STATE), hi_s.reshape(nb, SSM_GROUPS, SSM_STATE),
                        pool_s_t.transpose(1, 0, 2)])

    st_p = [jnp.stack(f) for f in zip(*new_p)]
    st_s = [jnp.stack(f) for f in zip(*new_s)]
    out = [x[:n_p].reshape(bp, lp, d), x[n_p:].reshape(nb, steps, d)]
    for a, b_ in zip(st_p, st_s):
        out += [a, b_]
    return tuple(out)
```

```python
import functools
import math

import jax
import jax.numpy as jnp
from jax import lax
from jax.experimental import pallas as pl
from jax.experimental.pallas import tpu as pltpu

F32 = jnp.float32
BF16 = jnp.bfloat16

D_MODEL = 2048
DEPTH = 2
PAGE_SIZE = 128
SSM_WIDTH = 512
SSM_GROUP_CH = 16
SSM_GROUPS = 32
SSM_STATE = 64
SSM_FLAT = SSM_GROUPS * SSM_STATE
POOL_WIDTH = 512
POOL_WINDOWS = (2, 4, 8, 16)
POOL_CH = 128
POOL_BUF = 15
NSA_WIDTH = 1024
HEAD_DIM = 64
N_HEADS = 16
N_KV = 4
GQA = 4
KV_WIDTH = N_KV * HEAD_DIM
CMP_STRIDE = 16
CMP_BLOCK = 32
CMP_HIDDEN = 128
SEL_BLOCK = 64
TOP_N = 16
WINDOW = 512
Q_BLOCK = 128
D_FF = 4 * D_MODEL
ALPHA = (2 * DEPTH) ** 0.25
LN_EPS = 1e-5
NEG_INF = -1e30
TINY = 1e-30
FORCED_SCORE = 1e4
PROJ_SIZES = (SSM_WIDTH, POOL_WIDTH, NSA_WIDTH) + (KV_WIDTH,) * 6 + (3 * N_HEADS,)
PROJ_WIDTH = sum(PROJ_SIZES)
PROJ_PAD = 3840
GATE_COL = 3584

LANES = 128
SUBLANES = 8
VMEM_LIMIT = 56 * 1024 * 1024


def _cparams(*sem):
    return pltpu.CompilerParams(dimension_semantics=sem, vmem_limit_bytes=VMEM_LIMIT)


def _gelu(x):
    return 0.5 * x * (1.0 + jnp.tanh(math.sqrt(2.0 / math.pi) * (x + 0.044715 * (x * x * x))))


def _sigmoid(x):
    return 1.0 / (1.0 + jnp.exp(-x))


def _layer_norm(z, g, b):
    zc = z - jnp.mean(z, axis=-1, keepdims=True)
    var = jnp.mean(zc * zc, axis=-1, keepdims=True)
    return zc * lax.rsqrt(var + LN_EPS) * g + b


def _proj_kernel(x_ref, w_ref, o_ref, ob_ref, xb_ref):
    @pl.when(pl.program_id(1) == 0)
    def _():
        xb_ref[...] = x_ref[...].astype(BF16)

    o = jnp.dot(xb_ref[...], w_ref[...].astype(BF16), preferred_element_type=F32)
    o_ref[...] = o
    ob_ref[...] = o.astype(BF16)


def _input_projection(x, w, layer, tm=1024, tn=768):
    m, k = x.shape
    n = w.shape[2]
    return pl.pallas_call(
        _proj_kernel,
        grid=(m // tm, n // tn),
        in_specs=[pl.BlockSpec((tm, k), lambda i, j: (i, 0)),
                  pl.BlockSpec((None, k, tn), lambda i, j: (layer, 0, j))],
        out_specs=[pl.BlockSpec((tm, tn), lambda i, j: (i, j)), pl.BlockSpec((tm, tn), lambda i, j: (i, j))],
        out_shape=[jax.ShapeDtypeStruct((m, n), F32), jax.ShapeDtypeStruct((m, n), BF16)],
        scratch_shapes=[pltpu.VMEM((tm, k), BF16)],
        compiler_params=_cparams("parallel", "arbitrary"),
        name="input_projection",
    )(x, w)


def _outproj_ln_kernel(x_ref, ysp_ref, yss_ref, ypp_ref, yps_ref, ynp_ref, yns_ref, w_ref, g_ref, b_ref, o_ref, wb_ref,
                       *, p_tiles):
    i = pl.program_id(0)

    @pl.when(i == 0)
    def _():
        wb_ref[...] = w_ref[...].astype(BF16)

    from_prompt = i < p_tiles
    ys = jnp.where(from_prompt, ysp_ref[...], yss_ref[...])
    yp = jnp.where(from_prompt, ypp_ref[...], yps_ref[...])
    yn = jnp.where(from_prompt, ynp_ref[...], yns_ref[...].astype(BF16))
    acc = jnp.dot(ys, wb_ref[0:SSM_WIDTH, :], preferred_element_type=F32)
    acc += jnp.dot(yp, wb_ref[SSM_WIDTH:SSM_WIDTH + POOL_WIDTH, :], preferred_element_type=F32)
    acc += jnp.dot(yn, wb_ref[SSM_WIDTH + POOL_WIDTH:, :], preferred_element_type=F32)
    o_ref[...] = _layer_norm(ALPHA * x_ref[...] + acc, g_ref[...], b_ref[...])


def _output_projection_ln(x, n_p, y_ssm, y_pool, y_nsa, w, layer, g, b, tm=512):
    m, d = x.shape
    p_tiles = n_p // tm
    row = lambda i: (i, 0)
    prow = lambda i: (jnp.minimum(i, p_tiles - 1), 0)
    srow = lambda i: (jnp.maximum(i - p_tiles, 0), 0)
    fixed = lambda i: (0, 0)
    pair = lambda width: [pl.BlockSpec((tm, width), prow), pl.BlockSpec((tm, width), srow)]
    return pl.pallas_call(
        functools.partial(_outproj_ln_kernel, p_tiles=p_tiles),
        grid=(m // tm,),
        in_specs=[pl.BlockSpec((tm, d), row)] + pair(SSM_WIDTH) + pair(POOL_WIDTH) + pair(NSA_WIDTH)
        + [pl.BlockSpec((None, d, d), lambda i: (layer, 0, 0), pipeline_mode=pl.Buffered(1)),
           pl.BlockSpec((1, d), fixed),
           pl.BlockSpec((1, d), fixed)],
        out_specs=pl.BlockSpec((tm, d), row),
        out_shape=jax.ShapeDtypeStruct((m, d), F32),
        scratch_shapes=[pltpu.VMEM((d, d), BF16)],
        compiler_params=_cparams("arbitrary"),
        name="output_projection_ln",
    )(x, *y_ssm, *y_pool, *y_nsa, w, g, b)


def _mlp_ln_kernel(x_ref, w1_ref, w2_ref, g_ref, b_ref, o_ref, xb_ref, acc_ref):
    f = pl.program_id(1)

    @pl.when(f == 0)
    def _():
        xb_ref[...] = x_ref[...].astype(BF16)
        acc_ref[...] = jnp.zeros_like(acc_ref)

    h = jnp.dot(xb_ref[...], w1_ref[...].astype(BF16), preferred_element_type=F32)
    h = jnp.square(jnp.maximum(h, 0.0)).astype(BF16)
    acc_ref[...] += jnp.dot(h, w2_ref[...].astype(BF16), preferred_element_type=F32)

    @pl.when(f == pl.num_programs(1) - 1)
    def _():
        o_ref[...] = _layer_norm(ALPHA * x_ref[...] + acc_ref[...], g_ref[...], b_ref[...])


def _mlp_ln(x, w1, w2, layer, g, b, tm=1024, tf=512):
    m, d = x.shape
    ff = w1.shape[2]
    once = pl.Buffered(1)
    return pl.pallas_call(
        _mlp_ln_kernel,
        grid=(m // tm, ff // tf),
        in_specs=[pl.BlockSpec((tm, d), lambda i, f: (i, 0), pipeline_mode=once),
                  pl.BlockSpec((None, d, tf), lambda i, f: (layer, 0, f)),
                  pl.BlockSpec((None, tf, d), lambda i, f: (layer, f, 0)),
                  pl.BlockSpec((1, d), lambda i, f: (0, 0)),
                  pl.BlockSpec((1, d), lambda i, f: (0, 0))],
        out_specs=pl.BlockSpec((tm, d), lambda i, f: (i, 0), pipeline_mode=once),
        out_shape=jax.ShapeDtypeStruct((m, d), F32),
        scratch_shapes=[pltpu.VMEM((tm, d), BF16), pltpu.VMEM((tm, d), F32)],
        compiler_params=_cparams("parallel", "arbitrary"),
        name="mlp_ln",
    )(x, w1, w2, g, b)


S5_GROUP_BLOCKS = 4
S5_BLOCK_LANES = SSM_FLAT // S5_GROUP_BLOCKS // LANES
S5_SLABS = SSM_FLAT // LANES
S5_SEGS = SUBLANES
S5_SEG_PAD = 8


def _s5_kernel(u_ref, bre_ref, bim_ref, cre_ref, cim_ref, vec_ref, pw_ref, d_ref, wglu_ref, h0re_ref, h0im_ref,
               y_ref, hre_out, him_out, xre_ref, xim_ref, hre_s, him_s, *, n_seq, steps, carry):
    u = u_ref[...]
    ub = u.astype(BF16)
    cw = SSM_WIDTH // S5_GROUP_BLOCKS
    sw = SSM_FLAT // S5_GROUP_BLOCKS
    seg_len = steps // S5_SEGS if carry else steps
    pitch = seg_len + S5_SEG_PAD if carry else steps
    n_par = S5_SEGS if carry else n_seq

    def put_rows(ref, slab, val):
        if carry:
            for sg in range(S5_SEGS):
                ref[slab, sg * pitch:sg * pitch + seg_len, :] = val[sg * seg_len:(sg + 1) * seg_len]
        else:
            ref[slab] = val

    def get_rows(ref, slab):
        if carry:
            return jnp.concatenate([ref[slab, sg * pitch:sg * pitch + seg_len, :] for sg in range(S5_SEGS)], axis=0)
        return ref[slab]

    for j in range(S5_GROUP_BLOCKS):
        uj = ub[:, j * cw:(j + 1) * cw]
        bur = jnp.dot(uj, bre_ref[j], preferred_element_type=F32)
        bui = jnp.dot(uj, bim_ref[j], preferred_element_type=F32)
        zr = vec_ref[2:3, j * sw:(j + 1) * sw]
        zi = vec_ref[3:4, j * sw:(j + 1) * sw]
        xr = zr * bur - zi * bui
        xi = zr * bui + zi * bur
        for q in range(S5_BLOCK_LANES):
            put_rows(xre_ref, j * S5_BLOCK_LANES + q, xr[:, q * LANES:(q + 1) * LANES])
            put_rows(xim_ref, j * S5_BLOCK_LANES + q, xi[:, q * LANES:(q + 1) * LANES])

    if carry:
        @pl.when(pl.program_id(0) == 0)
        def _():
            hre_s[...] = h0re_ref[...]
            him_s[...] = h0im_ref[...]
    else:
        hre_s[...] = h0re_ref[...]
        him_s[...] = h0im_ref[...]

    scan_slabs = 2 * S5_BLOCK_LANES if carry else S5_BLOCK_LANES
    for j in range(S5_SLABS // scan_slabs):
        slabs = [j * scan_slabs + q for q in range(scan_slabs)]
        lanes = [slice(s * LANES, (s + 1) * LANES) for s in slabs]
        ar = [jnp.broadcast_to(vec_ref[0:1, l], (n_par, LANES)) for l in lanes]
        ai = [jnp.broadcast_to(vec_ref[1:2, l], (n_par, LANES)) for l in lanes]

        def step(t, h, slabs=slabs, ar=ar, ai=ai):
            rows = pl.ds(t, n_par, stride=pitch)
            out = []
            for q, s in enumerate(slabs):
                hr, hi = h[2 * q], h[2 * q + 1]
                nr = ar[q] * hr - ai[q] * hi + xre_ref[s, rows, :]
                ni = ar[q] * hi + ai[q] * hr + xim_ref[s, rows, :]
                xre_ref[s, rows, :] = nr
                xim_ref[s, rows, :] = ni
                out += [nr, ni]
            return tuple(out)

        h0 = []
        for l in lanes:
            h0 += ([jnp.zeros((n_par, LANES), F32)] * 2 if carry else [hre_s[:, l], him_s[:, l]])
        hT = lax.fori_loop(0, seg_len, step, tuple(h0), unroll=8)

        if not carry:
            for q, l in enumerate(lanes):
                hre_s[:, l] = hT[2 * q]
                him_s[:, l] = hT[2 * q + 1]
            continue

        cre, cim = [], []
        for q, l in enumerate(lanes):
            a_r, a_i = pw_ref[0, slabs[q], seg_len - 1:seg_len, :], pw_ref[1, slabs[q], seg_len - 1:seg_len, :]
            c_r, c_i = hre_s[:, l], him_s[:, l]
            rows_r, rows_i = [], []
            for sg in range(S5_SEGS):
                rows_r.append(c_r)
                rows_i.append(c_i)
                e_r, e_i = hT[2 * q][sg:sg + 1], hT[2 * q + 1][sg:sg + 1]
                c_r, c_i = e_r + a_r * c_r - a_i * c_i, e_i + a_r * c_i + a_i * c_r
            hre_s[:, l] = c_r
            him_s[:, l] = c_i
            cre.append(jnp.concatenate(rows_r, axis=0))
            cim.append(jnp.concatenate(rows_i, axis=0))

        def fix(t, carry_, slabs=slabs, lanes=lanes, cre=cre, cim=cim):
            rows = pl.ds(t, n_par, stride=pitch)
            for q, s in enumerate(slabs):
                p_r, p_i = pw_ref[0, s, pl.ds(t, 1), :], pw_ref[1, s, pl.ds(t, 1), :]
                xre_ref[s, rows, :] = xre_ref[s, rows, :] + (p_r * cre[q] - p_i * cim[q])
                xim_ref[s, rows, :] = xim_ref[s, rows, :] + (p_r * cim[q] + p_i * cre[q])
            return carry_

        lax.fori_loop(0, seg_len, fix, 0, unroll=8)

    hre_out[...] = hre_s[...]
    him_out[...] = him_s[...]

    ys = []
    for j in range(S5_GROUP_BLOCKS):
        slabs = range(j * S5_BLOCK_LANES, (j + 1) * S5_BLOCK_LANES)
        hr = jnp.concatenate([get_rows(xre_ref, s) for s in slabs], axis=-1).astype(BF16)
        hi = jnp.concatenate([get_rows(xim_ref, s) for s in slabs], axis=-1).astype(BF16)
        ys.append(jnp.dot(hr, cre_ref[j], preferred_element_type=F32)
                  - jnp.dot(hi, cim_ref[j], preferred_element_type=F32))
    y = jnp.concatenate(ys, axis=-1) + d_ref[...] * u
    z = _gelu(y)
    gate = _sigmoid(jnp.dot(z.astype(BF16), wglu_ref[...], preferred_element_type=F32))
    y_ref[...] = (z * gate).astype(BF16)


def _s5_mixer(proj, row0, n_rows, n_seq, steps, carry, wts, h0_re, h0_im):
    bre, bim, cre, cim, vec, pw, d, wglu = wts
    chunk = n_seq * steps
    n_chunks = n_rows // chunk
    blk0 = row0 // chunk
    st_rows = h0_re.shape[0]
    st_map = (lambda i: (0, 0)) if carry else (lambda i: (i, 0))
    fixed2 = lambda i: (0, 0)
    fixed3 = lambda i: (0, 0, 0)
    scratch_rows = S5_SEGS * (steps // S5_SEGS + S5_SEG_PAD) if carry else chunk
    kern = functools.partial(_s5_kernel, n_seq=n_seq, steps=steps, carry=carry)
    return pl.pallas_call(
        kern,
        grid=(n_chunks,),
        in_specs=[pl.BlockSpec((chunk, SSM_WIDTH), lambda i: (blk0 + i, 0)),
                  pl.BlockSpec(bre.shape, fixed3), pl.BlockSpec(bim.shape, fixed3),
                  pl.BlockSpec(cre.shape, fixed3), pl.BlockSpec(cim.shape, fixed3),
                  pl.BlockSpec(vec.shape, fixed2), pl.BlockSpec(pw.shape, lambda i: (0, 0, 0, 0)),
                  pl.BlockSpec(d.shape, fixed2),
                  pl.BlockSpec(wglu.shape, fixed2),
                  pl.BlockSpec((n_seq, SSM_FLAT), st_map), pl.BlockSpec((n_seq, SSM_FLAT), st_map)],
        out_specs=[pl.BlockSpec((chunk, SSM_WIDTH), lambda i: (i, 0)),
                   pl.BlockSpec((n_seq, SSM_FLAT), st_map), pl.BlockSpec((n_seq, SSM_FLAT), st_map)],
        out_shape=[jax.ShapeDtypeStruct((n_rows, SSM_WIDTH), BF16),
                   jax.ShapeDtypeStruct((st_rows, SSM_FLAT), F32),
                   jax.ShapeDtypeStruct((st_rows, SSM_FLAT), F32)],
        scratch_shapes=[pltpu.VMEM((S5_SLABS, scratch_rows, LANES), F32),
                        pltpu.VMEM((S5_SLABS, scratch_rows, LANES), F32),
                        pltpu.VMEM((n_seq, SSM_FLAT), F32), pltpu.VMEM((n_seq, SSM_FLAT), F32)],
        compiler_params=_cparams("arbitrary"),
        name="s5_mixer_carry" if carry else "s5_mixer_batch",
    )(proj, bre, bim, cre, cim, vec, pw, d, wglu, h0_re, h0_im)


def _s5_weights(a_re, a_im, log_dt, b_re, b_im, c_re, c_im, d, w_glu, seg_len):
    dt = jnp.exp(log_dt)[:, None]
    mag = jnp.exp(a_re * dt)
    abar_re, abar_im = mag * jnp.cos(a_im * dt), mag * jnp.sin(a_im * dt)
    den = a_re * a_re + a_im * a_im
    zr = ((abar_re - 1.0) * a_re + abar_im * a_im) / den
    zi = (abar_im * a_re - (abar_re - 1.0) * a_im) / den
    flat = lambda v: v.reshape(1, SSM_FLAT)
    vec = jnp.concatenate([flat(abar_re), flat(abar_im), flat(zr), flat(zi),
                           jnp.zeros((SUBLANES - 4, SSM_FLAT), F32)], axis=0)
    p_re, p_im, pows_re, pows_im = flat(abar_re), flat(abar_im), [], []
    for _ in range(seg_len):
        pows_re.append(p_re)
        pows_im.append(p_im)
        p_re, p_im = p_re * flat(abar_re) - p_im * flat(abar_im), p_re * flat(abar_im) + p_im * flat(abar_re)
    tail = [jnp.zeros((SUBLANES, SSM_FLAT), F32)]
    slabbed = lambda rows: jnp.concatenate(rows + tail, axis=0).reshape(-1, S5_SLABS, LANES).transpose(1, 0, 2)
    pw = jnp.stack([slabbed(pows_re), slabbed(pows_im)])
    gb = SSM_GROUPS // S5_GROUP_BLOCKS
    eye = jnp.eye(gb, dtype=F32)

    def pack_b(b):
        bb = b.reshape(S5_GROUP_BLOCKS, gb, SSM_STATE, SSM_GROUP_CH)
        m = jnp.einsum('jgph,gk->jghkp', bb, eye)
        return m.reshape(S5_GROUP_BLOCKS, gb * SSM_GROUP_CH, gb * SSM_STATE).astype(BF16)

    def pack_c(c):
        cc = c.reshape(S5_GROUP_BLOCKS, gb, SSM_GROUP_CH, SSM_STATE)
        m = jnp.einsum('jghp,gk->jgpkh', cc, eye)
        return m.reshape(S5_GROUP_BLOCKS, gb * SSM_STATE, gb * SSM_GROUP_CH).astype(BF16)

    return (pack_b(b_re), pack_b(b_im), pack_c(c_re), pack_c(c_im), vec, pw, d.reshape(1, SSM_WIDTH),
            w_glu.astype(BF16))


POOL_HIST = 16


def _pool_seq_kernel(u_ref, buf_ref, w_ref, scale_ref, y_ref, new_ref, xc_ref, *, chunk, pos0):
    i = pl.program_id(0)

    @pl.when(i == 0)
    def _():
        xc_ref[0:POOL_HIST, :] = buf_ref[...]

    u = u_ref[...]
    xc_ref[POOL_HIST:POOL_HIST + chunk, :] = u
    pos = pos0 + i * chunk + lax.broadcasted_iota(jnp.int32, (chunk, POOL_CH), 0)
    outs = []
    for g, wd in enumerate(POOL_WINDOWS):
        lanes = slice(g * POOL_CH, (g + 1) * POOL_CH)
        ug = u[:, lanes]
        acc = ug
        for k in range(1, wd):
            acc = acc + xc_ref[pl.ds(POOL_HIST - k, chunk), lanes]
        mix = acc / jnp.minimum(wd, pos + 1).astype(F32) - ug
        outs.append(jnp.dot(mix.astype(BF16), w_ref[g], preferred_element_type=F32))
    y_ref[...] = (jnp.concatenate(outs, axis=-1) * scale_ref[...]).astype(BF16)
    tail = xc_ref[chunk:chunk + POOL_HIST, :]
    xc_ref[0:POOL_HIST, :] = tail
    new_ref[...] = tail


def _pool_mixer_seq(proj, row0, n_rows, pos0, buf16, w, scale, chunk=256):
    blk0 = row0 // chunk
    kern = functools.partial(_pool_seq_kernel, chunk=chunk, pos0=pos0)
    return pl.pallas_call(
        kern,
        grid=(n_rows // chunk,),
        in_specs=[pl.BlockSpec((chunk, POOL_WIDTH), lambda i: (blk0 + i, 1)),
                  pl.BlockSpec((POOL_HIST, POOL_WIDTH), lambda i: (0, 0)),
                  pl.BlockSpec(w.shape, lambda i: (0, 0, 0)),
                  pl.BlockSpec((1, POOL_WIDTH), lambda i: (0, 0))],
        out_specs=[pl.BlockSpec((chunk, POOL_WIDTH), lambda i: (i, 0)),
                   pl.BlockSpec((POOL_HIST, POOL_WIDTH), lambda i: (0, 0))],
        out_shape=[jax.ShapeDtypeStruct((n_rows, POOL_WIDTH), BF16),
                   jax.ShapeDtypeStruct((POOL_HIST, POOL_WIDTH), F32)],
        scratch_shapes=[pltpu.VMEM((POOL_HIST + chunk, POOL_WIDTH), F32)],
        compiler_params=_cparams("arbitrary"),
        name="pool_mixer_seq",
    )(proj, buf16, w, scale)


def _pool_batch_kernel(u_ref, buf_ref, w_ref, scale_ref, y_ref, new_ref, *, steps, pos0):
    def xrow(j, lanes):
        return buf_ref[j, :, lanes] if j < POOL_BUF else u_ref[j - POOL_BUF, :, lanes]

    nb = u_ref.shape[1]
    for g, wd in enumerate(POOL_WINDOWS):
        lanes = slice(g * POOL_CH, (g + 1) * POOL_CH)
        mixes = []
        for t in range(steps):
            ug = u_ref[t, :, lanes]
            acc = ug
            for k in range(1, wd):
                acc = acc + xrow(POOL_BUF + t - k, lanes)
            mixes.append(acc / float(min(wd, pos0 + t + 1)) - ug)
        mix = jnp.concatenate(mixes, axis=0).astype(BF16)
        yg = jnp.dot(mix, w_ref[g], preferred_element_type=F32) * scale_ref[:, lanes]
        for t in range(steps):
            y_ref[t, :, lanes] = yg[t * nb:(t + 1) * nb].astype(BF16)
    for j in range(POOL_BUF):
        new_ref[j] = xrow(steps + j, slice(None))


def _pool_mixer_batch(u_t, buf_t, pos0, w, scale):
    steps, nb, _ = u_t.shape
    kern = functools.partial(_pool_batch_kernel, steps=steps, pos0=pos0)
    full3 = lambda i: (0, 0, 0)
    return pl.pallas_call(
        kern,
        grid=(1,),
        in_specs=[pl.BlockSpec(u_t.shape, full3), pl.BlockSpec(buf_t.shape, full3),
                  pl.BlockSpec(w.shape, full3), pl.BlockSpec((1, POOL_WIDTH), lambda i: (0, 0))],
        out_specs=[pl.BlockSpec(u_t.shape, full3), pl.BlockSpec(buf_t.shape, full3)],
        out_shape=[jax.ShapeDtypeStruct(u_t.shape, BF16), jax.ShapeDtypeStruct(buf_t.shape, F32)],
        compiler_params=_cparams("arbitrary"),
        name="pool_mixer_batch",
    )(u_t, buf_t, w, scale)


CMP_HALVES = KV_WIDTH // LANES
CMP_PAIR = LANES // HEAD_DIM
CMP_K = CMP_STRIDE * LANES
CMP_N = 2 * CMP_PAIR * CMP_HIDDEN


def _compress_kernel(*refs, n_seq, n_blk, transposed):
    if transposed:
        refs = refs[1:]
    blk_refs = refs[:n_seq * n_blk]
    wpair_ref, pe_ref, w1_ref, w2_ref, out_ref, xs_ref, g_ref, b_ref = refs[n_seq * n_blk:]
    blk_rows = blk_refs[0].shape[1] if transposed else blk_refs[0].shape[0]
    n = n_blk * blk_rows // CMP_STRIDE
    total = n_seq * n
    for i, r in enumerate(blk_refs):
        x = r[...].T if transposed else r[...]
        for h in range(CMP_HALVES):
            xs_ref[h, i * blk_rows:(i + 1) * blk_rows, :] = x[:, h * LANES:(h + 1) * LANES]
    ab = []
    for h in range(CMP_HALVES):
        for s_ in range(CMP_STRIDE):
            g_ref[h, :, s_ * LANES:(s_ + 1) * LANES] = (
                xs_ref[h, pl.ds(s_, total, stride=CMP_STRIDE), :].astype(BF16))
        ab.append(jnp.dot(g_ref[h], wpair_ref[...], preferred_element_type=F32))
    hid0 = jnp.dot(pe_ref[...], w1_ref[...], preferred_element_type=F32)[0:1]
    hid0 = jnp.concatenate([hid0] * CMP_PAIR, axis=-1)
    half_n = CMP_N // 2
    row = lax.broadcasted_iota(jnp.int32, (n, LANES), 0)
    b_ref[n:n + SUBLANES, :] = jnp.zeros((SUBLANES, half_n), F32)
    for h in range(CMP_HALVES):
        for q in range(n_seq):
            b_ref[0:n, :] = ab[h][q * n:(q + 1) * n, half_n:]
            hid = ab[h][q * n:(q + 1) * n, :half_n] + b_ref[pl.ds(1, n), :] + hid0
            out = jnp.dot(_gelu(hid).astype(BF16), w2_ref[...], preferred_element_type=F32)
            out_ref[q, :, h * LANES:(h + 1) * LANES] = jnp.where(row < n - 1, out, 0.0).astype(out_ref.dtype)


def _compress_weights(pe, w1, w2):
    w1b = w1.reshape(2, CMP_STRIDE, HEAD_DIM, CMP_HIDDEN)
    eye = jnp.eye(CMP_PAIR, dtype=F32)
    wpair = jnp.einsum('asdh,kj->skdajh', w1b, eye).reshape(CMP_K, CMP_N).astype(BF16)
    w2pair = jnp.einsum('hd,kj->khjd', w2, eye).reshape(CMP_PAIR * CMP_HIDDEN, LANES).astype(BF16)
    pe8 = jnp.concatenate([pe.reshape(1, CMP_BLOCK * HEAD_DIM),
                           jnp.zeros((SUBLANES - 1, CMP_BLOCK * HEAD_DIM), F32)], axis=0).astype(BF16)
    return wpair, pe8, w1.astype(BF16), w2pair


def _compress_scratch(n_seq, rows, n):
    return [pltpu.VMEM((CMP_HALVES, n_seq * rows, LANES), F32),
            pltpu.VMEM((CMP_HALVES, n_seq * n, CMP_K), BF16),
            pltpu.VMEM((n + SUBLANES, CMP_N // 2), F32)]


def _compress_seq(proj, col_block, n_rows, wts):
    n = n_rows // CMP_STRIDE
    full = lambda i: (0, 0)
    return pl.pallas_call(
        functools.partial(_compress_kernel, n_seq=1, n_blk=1, transposed=False),
        grid=(1,),
        in_specs=[pl.BlockSpec((n_rows, KV_WIDTH), lambda i: (0, col_block))]
        + [pl.BlockSpec(w.shape, full) for w in wts],
        out_specs=pl.BlockSpec((1, n, KV_WIDTH), lambda i: (0, 0, 0)),
        out_shape=jax.ShapeDtypeStruct((1, n, KV_WIDTH), BF16),
        scratch_shapes=_compress_scratch(1, n_rows, n),
        compiler_params=_cparams("arbitrary"),
        name="compress_seq",
    )(proj, *wts)[0]


def _compress_paged(cache_t, layer, page_table, wts, group=2):
    nb, n_pages = page_table.shape
    n = n_pages * PAGE_SIZE // CMP_STRIDE
    full = lambda b, pt: (0, 0)

    def page_spec(bl, p):
        return pl.BlockSpec((None, None, KV_WIDTH, PAGE_SIZE), lambda b, pt: (layer, pt[group * b + bl, p], 0, 0))

    grid_spec = pltpu.PrefetchScalarGridSpec(
        num_scalar_prefetch=1,
        grid=(nb // group,),
        in_specs=[page_spec(bl, p) for bl in range(group) for p in range(n_pages)]
        + [pl.BlockSpec(w.shape, full) for w in wts],
        out_specs=pl.BlockSpec((group, n, KV_WIDTH), lambda b, pt: (b, 0, 0)),
        scratch_shapes=_compress_scratch(group, n_pages * PAGE_SIZE, n),
    )
    return pl.pallas_call(
        functools.partial(_compress_kernel, n_seq=group, n_blk=n_pages, transposed=True),
        grid_spec=grid_spec,
        out_shape=jax.ShapeDtypeStruct((nb, n, KV_WIDTH), BF16),
        compiler_params=_cparams("arbitrary"),
        name="compress_paged",
    )(page_table, *([cache_t] * (group * n_pages)), *wts)


def _alibi_slope(h):
    return 2.0 ** (-8.0 * (h + 1) / N_HEADS)


NT_DIMS = (((1,), (1,)), ((), ()))
QROWS = GQA * Q_BLOCK
KEY_TILE = 128
WIN_TILES = WINDOW // KEY_TILE
KEY_STEP = 2 * KEY_TILE
T_FULL, T_DIAG, T_OLD, T_NONE = 0, 1, 2, 3
LOG2E = math.log2(math.e)
V_ROWS = HEAD_DIM + 16
CMP_TILES_PER_SLAB = LANES * CMP_STRIDE // Q_BLOCK


def _nsa_seq_kernel(q_ref, gate_ref, kc_ref, vct_ref, ks_ref, vst_ref, kw_ref, vwt_ref, o_ref,
                    qx_ref, bias_ref, sbuf_ref, m_ref, acc_ref, oc_ref, os_ref, ow_ref, *, n_cmp_pad, n_sel, n_steps):
    i = pl.program_id(0)
    base = i * Q_BLOCK

    lane = lax.broadcasted_iota(jnp.int32, (1, QROWS), 1)
    qq_i = lane % Q_BLOCK
    qq = qq_i.astype(F32)
    g_lane = lane // Q_BLOCK

    def slope_row(k):
        r = jnp.full((1, QROWS), LOG2E * _alibi_slope(GQA * k + GQA - 1), F32)
        for g in range(GQA - 1):
            r = jnp.where(g_lane == g, LOG2E * _alibi_slope(GQA * k + g), r)
        return r

    slopes = [slope_row(k) for k in range(N_KV)]

    @pl.when(i == 0)
    def _():
        kk = lax.broadcasted_iota(jnp.int32, (KEY_TILE, QROWS), 0).astype(F32)
        for half in range(2):
            for k in range(N_KV):
                b = -slopes[k] * (qq - (kk + float(half * KEY_TILE)))
                bias_ref[half, T_FULL, k] = b
                bias_ref[half, T_DIAG, k] = jnp.where(kk <= qq, b, NEG_INF)
                bias_ref[half, T_OLD, k] = jnp.where(kk > qq, b, NEG_INF)
                bias_ref[half, T_NONE, k] = jnp.full((KEY_TILE, QROWS), NEG_INF, F32)

    qs = q_ref[...] * (LOG2E * HEAD_DIM ** -0.5)
    low_half = lax.broadcasted_iota(jnp.int32, (Q_BLOCK, LANES), 1) < HEAD_DIM
    for k in range(N_KV):
        keep = low_half if k % 2 == 0 else jnp.logical_not(low_half)
        parts = []
        for g in range(GQA):
            h = GQA * k + g
            pair = qs[:, (h // 2) * LANES:(h // 2 + 1) * LANES]
            if h % 2 != k % 2:
                pair = pltpu.roll(pair, HEAD_DIM, 1)
            parts.append(jnp.where(keep, pair, 0.0).astype(BF16))
        qx_ref[k, :, 0:LANES] = jnp.concatenate(parts, axis=0)

    sjf = lax.broadcasted_iota(jnp.int32, (n_sel, Q_BLOCK), 0).astype(F32)
    curf = ((base + lax.broadcasted_iota(jnp.int32, (1, Q_BLOCK), 1)) // SEL_BLOCK).astype(F32)
    visible = sjf <= curf
    forced = visible & ((sjf == 0.0) | (sjf == curf) | (sjf == curf - 1.0))

    def compressed_and_select(n_vis):
        n_io = lax.broadcasted_iota(jnp.int32, (n_vis, QROWS), 0)
        distc = (base.astype(F32) + qq) - (n_io.astype(F32) * CMP_STRIDE + (CMP_BLOCK - 1) / 2.0)
        visc = (n_io * CMP_STRIDE + (CMP_BLOCK - 1)) <= (base + qq_i)
        sj_o = lax.broadcasted_iota(jnp.int32, (n_sel, n_vis), 0)
        nn_o = lax.broadcasted_iota(jnp.int32, (n_sel, n_vis), 1)
        ovt = jnp.where((nn_o * CMP_STRIDE < (sj_o + 1) * SEL_BLOCK)
                        & (nn_o * CMP_STRIDE + (CMP_BLOCK - 1) >= sj_o * SEL_BLOCK), 1.0, 0.0).astype(BF16)
        for k in range(N_KV):
            kl = slice((k // 2) * LANES, (k // 2 + 1) * LANES)
            vr = slice(k * HEAD_DIM, (k + 1) * HEAD_DIM)
            s = lax.dot_general(kc_ref[0:n_vis, kl], qx_ref[k, :, 0:LANES], NT_DIMS, preferred_element_type=F32)
            s = jnp.where(visc, s - slopes[k] * distc, NEG_INF)
            m = jnp.maximum(jnp.max(s, axis=0, keepdims=True), 0.5 * NEG_INF)
            p = jnp.exp2(s - m)
            p = p * (1.0 / jnp.maximum(jnp.sum(p, axis=0, keepdims=True), TINY))
            oc_ref[k] = jnp.dot(vct_ref[vr, 0:n_vis], p.astype(BF16), preferred_element_type=F32)
            psum = p[:, 0:Q_BLOCK]
            for g in range(1, GQA):
                psum = psum + p[:, g * Q_BLOCK:(g + 1) * Q_BLOCK]
            p_hi = psum.astype(BF16)
            p_lo = (psum - p_hi.astype(F32)).astype(BF16)
            imp = (jnp.dot(ovt, p_hi, preferred_element_type=F32)
                   + jnp.dot(ovt, p_lo, preferred_element_type=F32))
            score = jnp.where(forced, FORCED_SCORE, jnp.where(visible, imp, -1.0))
            sel = jnp.zeros_like(score)
            for _ in range(min(TOP_N, n_sel)):
                mx = jnp.max(score, axis=0, keepdims=True)
                idx = jnp.min(jnp.where(score == mx, sjf, 1e9), axis=0, keepdims=True)
                hit = sjf == idx
                sel = jnp.where(hit & (mx >= 0.0), 1.0, sel)
                score = jnp.where(hit, -2.0, score)
            selq = jnp.where(sel > 0.0, 0.0, NEG_INF).T.astype(BF16)
            if n_sel < LANES:
                selq = jnp.concatenate([selq, jnp.zeros((Q_BLOCK, LANES - n_sel), BF16)], axis=1)
            qx_ref[k, :, LANES:2 * LANES] = jnp.concatenate([selq] * GQA, axis=0)

    n_slabs = -(-n_cmp_pad // LANES)
    for v in range(n_slabs):
        pl.when(i // CMP_TILES_PER_SLAB == v)(
            functools.partial(compressed_and_select, min((v + 1) * LANES, n_cmp_pad)))

    blk_of_key = lax.broadcasted_iota(jnp.int32, (KEY_STEP, LANES), 0) // SEL_BLOCK
    lane_id = lax.broadcasted_iota(jnp.int32, (KEY_STEP, LANES), 1)

    def flash(k_ref, vt_ref, j_lo, use_sel, out_ref):
        m_ref[...] = jnp.full(m_ref.shape, NEG_INF, F32)
        acc_ref[...] = jnp.zeros(acc_ref.shape, F32)

        def tile_type(t):
            ty = jnp.where(t == i, T_DIAG, jnp.where(t > i, T_NONE, T_FULL))
            if not use_sel:
                ty = jnp.where(t == i - WIN_TILES, T_OLD, jnp.where(t < i - WIN_TILES, T_NONE, ty))
            return ty

        def scores(j, heads):
            jj = jnp.minimum(j, n_steps - 1)
            ty0, ty1 = tile_type(2 * jj), tile_type(2 * jj + 1)
            if use_sel:
                onehot = jnp.where(lane_id == blk_of_key + (KEY_STEP // SEL_BLOCK) * jj, 1.0, 0.0).astype(BF16)
            out = []
            for k in heads:
                kl = slice((k // 2) * LANES, (k // 2 + 1) * LANES)
                if use_sel:
                    kx = jnp.concatenate([k_ref[jj, :, kl], onehot], axis=1)
                    s = lax.dot_general(kx, qx_ref[k], NT_DIMS, preferred_element_type=F32)
                else:
                    s = lax.dot_general(k_ref[jj, :, kl], qx_ref[k, :, 0:LANES], NT_DIMS,
                                        preferred_element_type=F32)
                out.append(s + jnp.concatenate([bias_ref[0, ty0, k], bias_ref[1, ty1, k]], axis=0))
            return out

        def softmax_pv(j, k, s):
            c = slopes[k] * (i * KEY_TILE - j * KEY_STEP).astype(F32)
            m_old = m_ref[k] + c
            m_new = jnp.maximum(m_old, jnp.max(s, axis=0, keepdims=True))
            alpha = jnp.exp2(m_old - m_new)
            p = jnp.exp2(s - m_new).astype(BF16)
            m_ref[k] = m_new - c
            acc_ref[k] = alpha * acc_ref[k] + jnp.dot(vt_ref[j, k * V_ROWS:(k + 1) * V_ROWS, :], p,
                                                      preferred_element_type=F32)

        sbuf_ref[0], sbuf_ref[1] = scores(j_lo, (0, 1))

        def body(j, carry):
            (s2,) = scores(j, (2,))
            softmax_pv(j, 0, sbuf_ref[0])
            (s3,) = scores(j, (3,))
            softmax_pv(j, 1, sbuf_ref[1])
            n0, n1 = scores(j + 1, (0, 1))
            sbuf_ref[0] = n0
            softmax_pv(j, 2, s2)
            sbuf_ref[1] = n1
            softmax_pv(j, 3, s3)
            return carry

        lax.fori_loop(j_lo, i // 2 + 1, body, 0)
        for k in range(N_KV):
            out_ref[k] = acc_ref[k, 0:HEAD_DIM] * (1.0 / jnp.maximum(acc_ref[k, HEAD_DIM:HEAD_DIM + 1], TINY))

    flash(ks_ref, vst_ref, 0, True, os_ref)
    flash(kw_ref, vwt_ref, jnp.maximum(i - WIN_TILES, 0) // 2, False, ow_ref)

    gate = _sigmoid(gate_ref[...])
    blocks = []
    for k in range(N_KV):
        for g in range(GQA):
            h = GQA * k + g
            sl = slice(g * Q_BLOCK, (g + 1) * Q_BLOCK)
            blocks.append(oc_ref[k, :, sl] * gate[3 * h:3 * h + 1]
                          + os_ref[k, :, sl] * gate[3 * h + 1:3 * h + 2]
                          + ow_ref[k, :, sl] * gate[3 * h + 2:3 * h + 3])
    o_ref[...] = jnp.concatenate(blocks, axis=0).T.astype(BF16)


def _value_steps(v):
    steps = v.shape[0] // KEY_STEP
    vt = v.reshape(steps, KEY_STEP, N_KV, HEAD_DIM).transpose(0, 2, 3, 1)
    ones = jnp.ones((steps, N_KV, 1, KEY_STEP), BF16)
    pad = jnp.zeros((steps, N_KV, V_ROWS - HEAD_DIM - 1, KEY_STEP), BF16)
    return jnp.concatenate([vt, ones, pad], axis=2).reshape(steps, N_KV * V_ROWS, KEY_STEP)


def _nsa_seq(proj, gate_t, kc, vct, ks_t, vst_t, kw_t, vwt_t, n_rows):
    n_tiles = n_rows // Q_BLOCK
    assert n_tiles % 2 == 0
    n_cmp_pad = kc.shape[0]
    n_sel = n_rows // SEL_BLOCK
    assert n_sel <= LANES
    kern = functools.partial(_nsa_seq_kernel, n_cmp_pad=n_cmp_pad, n_sel=n_sel, n_steps=n_tiles // 2)
    c2 = lambda i: (0, 0)
    c3 = lambda i: (0, 0, 0)
    st = (N_KV, HEAD_DIM, QROWS)
    return pl.pallas_call(
        kern,
        grid=(n_tiles,),
        in_specs=[pl.BlockSpec((Q_BLOCK, NSA_WIDTH), lambda i: (i, 1)),
                  pl.BlockSpec((3 * N_HEADS, Q_BLOCK), lambda i: (0, i)),
                  pl.BlockSpec(kc.shape, c2), pl.BlockSpec(vct.shape, c2),
                  pl.BlockSpec(ks_t.shape, c3), pl.BlockSpec(vst_t.shape, c3),
                  pl.BlockSpec(kw_t.shape, c3), pl.BlockSpec(vwt_t.shape, c3)],
        out_specs=pl.BlockSpec((Q_BLOCK, NSA_WIDTH), lambda i: (i, 0)),
        out_shape=jax.ShapeDtypeStruct((n_rows, NSA_WIDTH), BF16),
        scratch_shapes=[pltpu.VMEM((N_KV, QROWS, 2 * LANES), BF16),
                        pltpu.VMEM((2, 4, N_KV, KEY_TILE, QROWS), F32),
                        pltpu.VMEM((2, KEY_STEP, QROWS), F32),
                        pltpu.VMEM((N_KV, 1, QROWS), F32), pltpu.VMEM((N_KV, V_ROWS, QROWS), F32),
                        pltpu.VMEM(st, F32), pltpu.VMEM(st, F32), pltpu.VMEM(st, F32)],
        compiler_params=_cparams("arbitrary"),
        name="nsa_seq",
    )(proj, gate_t, kc, vct, ks_t, vst_t, kw_t, vwt_t)


def _nsa_batch_kernel(pt_ref, *refs, n_pages, steps, pos0, group):
    del pt_ref
    (kwin_ref, vwin_ref, kc_ref, vc_ref, q_ref, ksn_ref, vsn_ref, kwn_ref, vwn_ref, gate_ref,
     o_ref, kwo_ref, vwo_ref, bsel_ref, bwin_ref, bcmp_ref, e_ref, ov_ref) = refs[2 * group * n_pages:]
    rows = N_HEADS * steps
    past = n_pages * PAGE_SIZE
    nk = past + KEY_TILE
    nw = WINDOW + KEY_TILE

    @pl.when(pl.program_id(0) == 0)
    def _():
        def tables(width):
            r = lax.broadcasted_iota(jnp.int32, (rows, width), 0)
            c = lax.broadcasted_iota(jnp.int32, (rows, width), 1)
            slope = jnp.exp((-8.0 * math.log(2.0) / N_HEADS) * (r // steps + 1).astype(F32))
            return slope, pos0 + r % steps, c

        slope, qpos, key = tables(nk)
        bsel_ref[...] = jnp.where(key <= qpos, -slope * (qpos - key).astype(F32), NEG_INF)
        slope, qpos, w = tables(nw)
        kpos = pos0 - WINDOW + w
        d = qpos - kpos
        bwin_ref[...] = jnp.where((d >= 0) & (d < WINDOW) & (kpos >= 0), -slope * d.astype(F32), NEG_INF)
        slope, qpos, n = tables(LANES)
        c_mid = n.astype(F32) * CMP_STRIDE + (CMP_BLOCK - 1) / 2.0
        bcmp_ref[...] = jnp.where(n * CMP_STRIDE + (CMP_BLOCK - 1) <= qpos,
                                  -slope * (qpos.astype(F32) - c_mid), NEG_INF)
        sj = lax.broadcasted_iota(jnp.int32, (LANES, nk), 0)
        key = lax.broadcasted_iota(jnp.int32, (LANES, nk), 1)
        e_ref[...] = jnp.where(key // SEL_BLOCK == sj, 1.0, 0.0).astype(BF16)
        n = lax.broadcasted_iota(jnp.int32, (LANES, LANES), 0)
        sj = lax.broadcasted_iota(jnp.int32, (LANES, LANES), 1)
        ov_ref[...] = jnp.where((n * CMP_STRIDE < (sj + 1) * SEL_BLOCK)
                                & (n * CMP_STRIDE + (CMP_BLOCK - 1) >= sj * SEL_BLOCK), 1.0, 0.0).astype(BF16)

    def _nsa_batch_one(bl):
        ks_pages = refs[bl * n_pages:(bl + 1) * n_pages]
        vs_pages = refs[(group + bl) * n_pages:(group + bl + 1) * n_pages]
        tok = slice(bl * steps, (bl + 1) * steps)
        qs = q_ref[tok, :] * (HEAD_DIM ** -0.5)
        low_half = lax.broadcasted_iota(jnp.int32, (steps, LANES), 1) < HEAD_DIM
        zero = jnp.zeros((steps, LANES), F32)
        pieces = []
        for k in range(N_KV):
            keep = low_half if k % 2 == 0 else jnp.logical_not(low_half)
            for g in range(GQA):
                h = GQA * k + g
                pair = qs[:, (h // 2) * LANES:(h // 2 + 1) * LANES]
                if h % 2 != k % 2:
                    pair = pltpu.roll(pair, HEAD_DIM, 1)
                blk = jnp.where(keep, pair, 0.0)
                pieces.append(jnp.concatenate([blk, zero] if k // 2 == 0 else [zero, blk], axis=1))
        qb = jnp.concatenate(pieces, axis=0).astype(BF16)

        def new_tile(ref):
            return jnp.concatenate([ref[tok, :], jnp.zeros((KEY_TILE - steps, KV_WIDTH), F32)], axis=0).astype(BF16)

        def softmax_pv(s, v_parts, transposed):
            m = jnp.max(s, axis=-1, keepdims=True)
            p = jnp.exp(s - m)
            l = jnp.sum(p, axis=-1, keepdims=True)
            acc = None
            off = 0
            for v, v_t in zip(v_parts, transposed):
                n = v.shape[1] if v_t else v.shape[0]
                pb = p[:, off:off + n].astype(BF16)
                part = (lax.dot_general(pb, v, NT_DIMS, preferred_element_type=F32) if v_t
                        else jnp.dot(pb, v, preferred_element_type=F32))
                acc = part if acc is None else acc + part
                off += n
            return acc * (1.0 / jnp.maximum(l, TINY))

        bc = bcmp_ref[...]
        s_c = lax.dot_general(qb, kc_ref[bl], NT_DIMS, preferred_element_type=F32) + bc
        kt_all = jnp.concatenate([r[...] for r in ks_pages], axis=1).astype(BF16)
        s_s = jnp.concatenate([jnp.dot(qb, kt_all, preferred_element_type=F32),
                               lax.dot_general(qb, new_tile(ksn_ref), NT_DIMS, preferred_element_type=F32)],
                              axis=1) + bsel_ref[...]
        s_w = jnp.concatenate([jnp.dot(qb, kwin_ref[bl].astype(BF16), preferred_element_type=F32),
                               lax.dot_general(qb, new_tile(kwn_ref), NT_DIMS, preferred_element_type=F32)],
                              axis=1) + bwin_ref[...]
        yield

        m = jnp.max(s_c, axis=-1, keepdims=True)
        p = jnp.exp(s_c - m) * jnp.where(bc > 0.5 * NEG_INF, 1.0, 0.0)
        p = p * (1.0 / jnp.maximum(jnp.sum(p, axis=-1, keepdims=True), TINY))
        o_c = jnp.dot(p.astype(BF16), vc_ref[bl], preferred_element_type=F32)
        grp = GQA * steps
        psum = []
        for k in range(N_KV):
            acc = p[k * grp:k * grp + steps]
            for g in range(1, GQA):
                acc = acc + p[k * grp + g * steps:k * grp + (g + 1) * steps]
            psum.append(acc)
        psum = jnp.concatenate(psum, axis=0)
        p_hi = psum.astype(BF16)
        p_lo = (psum - p_hi.astype(F32)).astype(BF16)
        imp = (jnp.dot(p_hi, ov_ref[...], preferred_element_type=F32)
               + jnp.dot(p_lo, ov_ref[...], preferred_element_type=F32))
        yield

        o_w = softmax_pv(s_w, [vwin_ref[bl].astype(BF16), new_tile(vwn_ref)], [True, False])
        lane_w = lax.broadcasted_iota(jnp.int32, (KV_WIDTH, WINDOW), 1)
        for src, nw_ref, dst in ((kwin_ref, kwn_ref, kwo_ref), (vwin_ref, vwn_ref, vwo_ref)):
            new_t = jnp.concatenate([nw_ref[tok, :], jnp.zeros((LANES - steps, KV_WIDTH), F32)], axis=0).T
            new_t = jnp.concatenate([new_t] * (WINDOW // LANES), axis=1)
            shifted = pltpu.roll(src[bl], WINDOW - steps, 1)
            dst[bl] = jnp.where(lane_w < WINDOW - steps, shifted, pltpu.roll(new_t, WINDOW - steps, 1))
        yield

        sjf = lax.broadcasted_iota(jnp.int32, imp.shape, 1).astype(F32)
        step_of_row = lax.broadcasted_iota(jnp.int32, imp.shape, 0) % steps
        curf = ((pos0 + step_of_row) // SEL_BLOCK).astype(F32)
        visible = sjf <= curf
        forced = visible & ((sjf == 0.0) | (sjf == curf) | (sjf == curf - 1.0))
        score = jnp.where(forced, FORCED_SCORE, jnp.where(visible, imp, -1.0))
        n_blocks = -(-(past + steps) // SEL_BLOCK)
        rank = jnp.zeros_like(score)
        for jp in range(n_blocks):
            other = jnp.broadcast_to(score[:, jp:jp + 1], score.shape)
            beats = (other > score) | ((other == score) & (sjf > float(jp)))
            rank = rank + jnp.where(beats, 1.0, 0.0)
        sel = jnp.where((rank < float(TOP_N)) & (score >= 0.0), 1.0, 0.0)
        selk = jnp.dot(sel.astype(BF16), e_ref[...], preferred_element_type=F32)
        yield

        selb = jnp.where(selk > 0.5, 0.0, NEG_INF)
        sb = jnp.concatenate([selb[k * steps:(k + 1) * steps] for k in range(N_KV) for _ in range(GQA)], axis=0)
        vt_all = jnp.concatenate([r[...] for r in vs_pages], axis=1).astype(BF16)
        o_s = softmax_pv(s_s + sb, [vt_all, new_tile(vsn_ref)], [True, False])
        yield

        gates = _sigmoid(gate_ref[bl])
        y = o_c * gates[:, 0:1] + o_s * gates[:, 1:2] + o_w * gates[:, 2:3]
        for h in range(N_HEADS):
            k = h // GQA
            o_ref[tok, h * HEAD_DIM:(h + 1) * HEAD_DIM] = y[h * steps:(h + 1) * steps, k * HEAD_DIM:(k + 1) * HEAD_DIM]

    pending = [_nsa_batch_one(bl) for bl in range(group)]
    while pending:
        pending = [g for g in pending if next(g, True) is None]


def _nsa_batch(proj, row0, steps, pos0, gate_rows, kc, vc, ks_cache, vs_cache, layer, page_table, kwin, vwin,
               group=2):
    nb, n_pages = page_table.shape
    rows = N_HEADS * steps
    nk = n_pages * PAGE_SIZE + KEY_TILE
    tok = group * steps
    blk0 = row0 // tok
    kern = functools.partial(_nsa_batch_kernel, n_pages=n_pages, steps=steps, pos0=pos0, group=group)

    def page_spec(bl, p):
        return pl.BlockSpec((None, None, KV_WIDTH, PAGE_SIZE), lambda b, pt: (layer, pt[group * b + bl, p], 0, 0))

    pages = [page_spec(bl, p) for bl in range(group) for p in range(n_pages)]

    per_b = lambda b, pt: (b, 0, 0)
    kv_col0 = sum(PROJ_SIZES[:3]) // KV_WIDTH

    def new_spec(j):
        return pl.BlockSpec((tok, KV_WIDTH), lambda b, pt: (blk0 + b, kv_col0 + j))

    grid_spec = pltpu.PrefetchScalarGridSpec(
        num_scalar_prefetch=1,
        grid=(nb // group,),
        in_specs=pages + pages
        + [pl.BlockSpec((None, group, KV_WIDTH, WINDOW), lambda b, pt: (layer, b, 0, 0)),
           pl.BlockSpec((None, group, KV_WIDTH, WINDOW), lambda b, pt: (layer, b, 0, 0)),
           pl.BlockSpec((group,) + kc.shape[1:], per_b), pl.BlockSpec((group,) + vc.shape[1:], per_b),
           pl.BlockSpec((tok, NSA_WIDTH), lambda b, pt: (blk0 + b, 1)),
           new_spec(2), new_spec(3), new_spec(4), new_spec(5),
           pl.BlockSpec((group, rows, SUBLANES), per_b)],
        out_specs=[pl.BlockSpec((tok, NSA_WIDTH), lambda b, pt: (b, 0)),
                   pl.BlockSpec((group, KV_WIDTH, WINDOW), per_b), pl.BlockSpec((group, KV_WIDTH, WINDOW), per_b)],
        scratch_shapes=[pltpu.VMEM((rows, nk), F32), pltpu.VMEM((rows, WINDOW + KEY_TILE), F32),
                        pltpu.VMEM((rows, LANES), F32), pltpu.VMEM((LANES, nk), BF16),
                        pltpu.VMEM((LANES, LANES), BF16)],
    )
    return pl.pallas_call(
        kern,
        grid_spec=grid_spec,
        out_shape=[jax.ShapeDtypeStruct((nb * steps, NSA_WIDTH), F32),
                   jax.ShapeDtypeStruct((nb, KV_WIDTH, WINDOW), F32),
                   jax.ShapeDtypeStruct((nb, KV_WIDTH, WINDOW), F32)],
        compiler_params=_cparams("arbitrary"),
        name="nsa_batch",
    )(page_table, *([ks_cache] * (group * n_pages)), *([vs_cache] * (group * n_pages)), kwin, vwin, kc, vc,
      proj, proj, proj, proj, proj, gate_rows)


S5_SEQ_CHUNK = 256
S5_BATCH_SEQS = 16


def kernel(x_prompt, x_sample, cache_k_cmp, cache_v_cmp, cache_k_sel, cache_v_sel, cache_k_win, cache_v_win, state_ssm_re, state_ssm_im, state_pool, page_table, w_in, ssm_a_re, ssm_a_im, ssm_log_dt, ssm_b_re, ssm_b_im, ssm_c_re, ssm_c_im, ssm_d, ssm_w_glu, pool_w, pool_scale, cmp_pe, cmp_w1, cmp_w2, w_out, ln1_g, ln1_b, mlp_w1, mlp_w2, ln2_g, ln2_b):
    bp, lp, d = x_prompt.shape
    nb, steps, _ = x_sample.shape
    assert bp == 1 and d == D_MODEL and steps < CMP_STRIDE
    n_p, n_s = bp * lp, nb * steps
    n_phys = cache_k_cmp.shape[1]
    past_len = page_table.shape[1] * PAGE_SIZE
    assert cache_k_win.shape[2] == WINDOW and past_len >= WINDOW

    x = jnp.concatenate([x_prompt.reshape(n_p, d), x_sample.reshape(n_s, d)], axis=0)
    rows_last = lambda c: c.transpose(0, 1, 3, 4, 2).reshape(c.shape[0], c.shape[1], KV_WIDTH, c.shape[2])
    kcmp_pages, vcmp_pages = rows_last(cache_k_cmp), rows_last(cache_v_cmp)
    ksel_pages, vsel_pages = rows_last(cache_k_sel), rows_last(cache_v_sel)
    kwin_t, vwin_t = rows_last(cache_k_win), rows_last(cache_v_win)
    kv_col0 = sum(PROJ_SIZES[:3])
    zero_state = jnp.zeros((1, SSM_FLAT), F32)
    zero_buf = jnp.zeros((POOL_HIST, POOL_WIDTH), F32)
    row = lambda v: v.reshape(1, -1)

    w_in_pad = jnp.pad(w_in, ((0, 0), (0, 0), (0, PROJ_PAD - PROJ_WIDTH)))

    new_p, new_s = [], []
    for l in range(DEPTH):
        proj, proj_b = _input_projection(x, w_in_pad, l)
        kv = [proj[:, kv_col0 + j * KV_WIDTH:kv_col0 + (j + 1) * KV_WIDTH] for j in range(6)]
        kv_b = [proj_b[:n_p, kv_col0 + j * KV_WIDTH:kv_col0 + (j + 1) * KV_WIDTH] for j in range(6)]
        gate_logits = proj[:, GATE_COL:GATE_COL + 3 * N_HEADS]

        s5w = _s5_weights(ssm_a_re[l], ssm_a_im[l], ssm_log_dt[l], ssm_b_re[l], ssm_b_im[l], ssm_c_re[l],
                          ssm_c_im[l], ssm_d[l], ssm_w_glu[l], S5_SEQ_CHUNK // S5_SEGS)
        ys_p, hr_p, hi_p = _s5_mixer(proj, 0, n_p, 1, S5_SEQ_CHUNK, True, s5w, zero_state, zero_state)
        ys_s, hr_s, hi_s = _s5_mixer(proj, n_p, n_s, S5_BATCH_SEQS, steps, False, s5w,
                                     state_ssm_re[l].reshape(nb, SSM_FLAT), state_ssm_im[l].reshape(nb, SSM_FLAT))

        pw, psc = pool_w[l].astype(BF16), row(pool_scale[l])
        yp_p, pool_p = _pool_mixer_seq(proj, 0, n_p, 0, zero_buf, pw, psc)
        u_t = proj[n_p:, SSM_WIDTH:SSM_WIDTH + POOL_WIDTH].reshape(nb, steps, POOL_WIDTH).transpose(1, 0, 2)
        yp_s_t, pool_s_t = _pool_mixer_batch(u_t, state_pool[l].transpose(1, 0, 2), past_len, pw, psc)
        yp_s = yp_s_t.transpose(1, 0, 2).reshape(n_s, POOL_WIDTH)

        cw_k = _compress_weights(cmp_pe[l, 0], cmp_w1[l, 0], cmp_w2[l, 0])
        cw_v = _compress_weights(cmp_pe[l, 1], cmp_w1[l, 1], cmp_w2[l, 1])
        kc_p = _compress_seq(proj, kv_col0 // KV_WIDTH, n_p, cw_k)
        vc_p = _compress_seq(proj, kv_col0 // KV_WIDTH + 1, n_p, cw_v)
        tiles = lambda a: a.reshape(n_p // KEY_STEP, KEY_STEP, KV_WIDTH)
        yn_p = _nsa_seq(proj, gate_logits[:n_p].T, kc_p, vc_p.T, tiles(kv_b[2]), _value_steps(kv_b[3]),
                        tiles(kv_b[4]), _value_steps(kv_b[5]), n_p)

        kc_s = _compress_paged(kcmp_pages, l, page_table, cw_k)
        vc_s = _compress_paged(vcmp_pages, l, page_table, cw_v)
        gate_rows = gate_logits[n_p:].reshape(nb, steps, N_HEADS, 3).transpose(0, 2, 1, 3)
        gate_rows = jnp.pad(gate_rows.reshape(nb, N_HEADS * steps, 3), ((0, 0), (0, 0), (0, SUBLANES - 3)))
        yn_s, kw_s, vw_s = _nsa_batch(proj, n_p, steps, past_len, gate_rows, kc_s, vc_s, ksel_pages, vsel_pages, l,
                                      page_table, kwin_t, vwin_t)

        x = _output_projection_ln(x, n_p, (ys_p, ys_s), (yp_p, yp_s), (yn_p, yn_s), w_out, l,
                                  row(ln1_g[l]), row(ln1_b[l]))
        x = _mlp_ln(x, mlp_w1, mlp_w2, l, row(ln2_g[l]), row(ln2_b[l]))

        heads = lambda a, b_, t: a.reshape(b_, t, N_KV, HEAD_DIM)
        rows_first = lambda a: a.reshape(nb, N_KV, HEAD_DIM, WINDOW).transpose(0, 3, 1, 2)
        n_keep = min(WINDOW, lp)
        new_p.append([heads(kv[j][:n_p], bp, lp) for j in range(4)]
                     + [heads(kv[j][n_p - n_keep:n_p], bp, n_keep) for j in (4, 5)]
                     + [hr_p.reshape(bp, SSM_GROUPS, SSM_STATE), hi_p.reshape(bp, SSM_GROUPS, SSM_STATE),
                        pool_p[POOL_HIST - POOL_BUF:].reshape(bp, POOL_BUF, POOL_WIDTH)])
        new_s.append([heads(kv[j][n_p:], nb, steps) for j in range(4)]
                     + [rows_first(kw_s), rows_first(vw_s)]
                     + [hr_s.reshape(nb, SSM_GROUPS, SSM_STATE), hi_s.reshape(nb, SSM_GROUPS, SSM_STATE),
                        pool_s_t.transpose(1, 0, 2)])

    st_p = [jnp.stack(f) for f in zip(*new_p)]
    st_s = [jnp.stack(f) for f in zip(*new_s)]
    out = [x[:n_p].reshape(bp, lp, d), x[n_p:].reshape(nb, steps, d)]
    for a, b_ in zip(st_p, st_s):
        out += [a, b_]
    return tuple(out)
```

```python
import functools
import math

import jax
import jax.numpy as jnp
from jax import lax
from jax.experimental import pallas as pl
from jax.experimental.pallas import tpu as pltpu

F32 = jnp.float32
BF16 = jnp.bfloat16

D_MODEL = 2048
DEPTH = 2
PAGE_SIZE = 128
SSM_WIDTH = 512
SSM_GROUP_CH = 16
SSM_GROUPS = 32
SSM_STATE = 64
SSM_FLAT = SSM_GROUPS * SSM_STATE
POOL_WIDTH = 512
POOL_WINDOWS = (2, 4, 8, 16)
POOL_CH = 128
POOL_BUF = 15
NSA_WIDTH = 1024
HEAD_DIM = 64
N_HEADS = 16
N_KV = 4
GQA = 4
KV_WIDTH = N_KV * HEAD_DIM
CMP_STRIDE = 16
CMP_BLOCK = 32
CMP_HIDDEN = 128
SEL_BLOCK = 64
TOP_N = 16
WINDOW = 512
Q_BLOCK = 128
D_FF = 4 * D_MODEL
ALPHA = (2 * DEPTH) ** 0.25
LN_EPS = 1e-5
NEG_INF = -1e30
TINY = 1e-30
FORCED_SCORE = 1e4
PROJ_SIZES = (SSM_WIDTH, POOL_WIDTH, NSA_WIDTH) + (KV_WIDTH,) * 6 + (3 * N_HEADS,)
PROJ_WIDTH = sum(PROJ_SIZES)
PROJ_PAD = 3840
GATE_COL = 3584

LANES = 128
SUBLANES = 8
VMEM_LIMIT = 56 * 1024 * 1024


def _cparams(*sem):
    return pltpu.CompilerParams(dimension_semantics=sem, vmem_limit_bytes=VMEM_LIMIT)


def _gelu(x):
    return 0.5 * x * (1.0 + jnp.tanh(math.sqrt(2.0 / math.pi) * (x + 0.044715 * (x * x * x))))


def _sigmoid(x):
    return 1.0 / (1.0 + jnp.exp(-x))


def _layer_norm(z, g, b):
    zc = z - jnp.mean(z, axis=-1, keepdims=True)
    var = jnp.mean(zc * zc, axis=-1, keepdims=True)
    return zc * lax.rsqrt(var + LN_EPS) * g + b


def _proj_kernel(x_ref, w_ref, o_ref, ob_ref, xb_ref):
    @pl.when(pl.program_id(1) == 0)
    def _():
        xb_ref[...] = x_ref[...].astype(BF16)

    o = jnp.dot(xb_ref[...], w_ref[...].astype(BF16), preferred_element_type=F32)
    o_ref[...] = o
    ob_ref[...] = o.astype(BF16)


def _input_projection(x, w, layer, tm=1024, tn=768):
    m, k = x.shape
    n = w.shape[2]
    return pl.pallas_call(
        _proj_kernel,
        grid=(m // tm, n // tn),
        in_specs=[pl.BlockSpec((tm, k), lambda i, j: (i, 0)),
                  pl.BlockSpec((None, k, tn), lambda i, j: (layer, 0, j))],
        out_specs=[pl.BlockSpec((tm, tn), lambda i, j: (i, j)), pl.BlockSpec((tm, tn), lambda i, j: (i, j))],
        out_shape=[jax.ShapeDtypeStruct((m, n), F32), jax.ShapeDtypeStruct((m, n), BF16)],
        scratch_shapes=[pltpu.VMEM((tm, k), BF16)],
        compiler_params=_cparams("parallel", "arbitrary"),
        name="input_projection",
    )(x, w)


def _outproj_ln_kernel(x_ref, ysp_ref, yss_ref, ypp_ref, yps_ref, ynp_ref, yns_ref, w_ref, g_ref, b_ref, o_ref, wb_ref,
                       *, p_tiles):
    i = pl.program_id(0)

    @pl.when(i == 0)
    def _():
        wb_ref[...] = w_ref[...].astype(BF16)

    from_prompt = i < p_tiles
    ys = jnp.where(from_prompt, ysp_ref[...], yss_ref[...])
    yp = jnp.where(from_prompt, ypp_ref[...], yps_ref[...])
    yn = jnp.where(from_prompt, ynp_ref[...], yns_ref[...].astype(BF16))
    acc = jnp.dot(ys, wb_ref[0:SSM_WIDTH, :], preferred_element_type=F32)
    acc += jnp.dot(yp, wb_ref[SSM_WIDTH:SSM_WIDTH + POOL_WIDTH, :], preferred_element_type=F32)
    acc += jnp.dot(yn, wb_ref[SSM_WIDTH + POOL_WIDTH:, :], preferred_element_type=F32)
    o_ref[...] = _layer_norm(ALPHA * x_ref[...] + acc, g_ref[...], b_ref[...])


def _output_projection_ln(x, n_p, y_ssm, y_pool, y_nsa, w, layer, g, b, tm=512):
    m, d = x.shape
    p_tiles = n_p // tm
    row = lambda i: (i, 0)
    prow = lambda i: (jnp.minimum(i, p_tiles - 1), 0)
    srow = lambda i: (jnp.maximum(i - p_tiles, 0), 0)
    fixed = lambda i: (0, 0)
    pair = lambda width: [pl.BlockSpec((tm, width), prow), pl.BlockSpec((tm, width), srow)]
    return pl.pallas_call(
        functools.partial(_outproj_ln_kernel, p_tiles=p_tiles),
        grid=(m // tm,),
        in_specs=[pl.BlockSpec((tm, d), row)] + pair(SSM_WIDTH) + pair(POOL_WIDTH) + pair(NSA_WIDTH)
        + [pl.BlockSpec((None, d, d), lambda i: (layer, 0, 0), pipeline_mode=pl.Buffered(1)),
           pl.BlockSpec((1, d), fixed),
           pl.BlockSpec((1, d), fixed)],
        out_specs=pl.BlockSpec((tm, d), row),
        out_shape=jax.ShapeDtypeStruct((m, d), F32),
        scratch_shapes=[pltpu.VMEM((d, d), BF16)],
        compiler_params=_cparams("arbitrary"),
        name="output_projection_ln",
    )(x, *y_ssm, *y_pool, *y_nsa, w, g, b)


def _mlp_ln_kernel(x_ref, w1_ref, w2_ref, g_ref, b_ref, o_ref, xb_ref, acc_ref):
    f = pl.program_id(1)

    @pl.when(f == 0)
    def _():
        xb_ref[...] = x_ref[...].astype(BF16)
        acc_ref[...] = jnp.zeros_like(acc_ref)

    h = jnp.dot(xb_ref[...], w1_ref[...].astype(BF16), preferred_element_type=F32)
    h = jnp.square(jnp.maximum(h, 0.0)).astype(BF16)
    acc_ref[...] += jnp.dot(h, w2_ref[...].astype(BF16), preferred_element_type=F32)

    @pl.when(f == pl.num_programs(1) - 1)
    def _():
        o_ref[...] = _layer_norm(ALPHA * x_ref[...] + acc_ref[...], g_ref[...], b_ref[...])


def _mlp_ln(x, w1, w2, layer, g, b, tm=1024, tf=512):
    m, d = x.shape
    ff = w1.shape[2]
    once = pl.Buffered(1)
    return pl.pallas_call(
        _mlp_ln_kernel,
        grid=(m // tm, ff // tf),
        in_specs=[pl.BlockSpec((tm, d), lambda i, f: (i, 0), pipeline_mode=once),
                  pl.BlockSpec((None, d, tf), lambda i, f: (layer, 0, f)),
                  pl.BlockSpec((None, tf, d), lambda i, f: (layer, f, 0)),
                  pl.BlockSpec((1, d), lambda i, f: (0, 0)),
                  pl.BlockSpec((1, d), lambda i, f: (0, 0))],
        out_specs=pl.BlockSpec((tm, d), lambda i, f: (i, 0), pipeline_mode=once),
        out_shape=jax.ShapeDtypeStruct((m, d), F32),
        scratch_shapes=[pltpu.VMEM((tm, d), BF16), pltpu.VMEM((tm, d), F32)],
        compiler_params=_cparams("parallel", "arbitrary"),
        name="mlp_ln",
    )(x, w1, w2, g, b)


S5_GROUP_BLOCKS = 4
S5_BLOCK_LANES = SSM_FLAT // S5_GROUP_BLOCKS // LANES
S5_SLABS = SSM_FLAT // LANES
S5_SEGS = SUBLANES
S5_SEG_PAD = 8


def _s5_kernel(u_ref, bre_ref, bim_ref, cre_ref, cim_ref, vec_ref, pw_ref, d_ref, wglu_ref, h0re_ref, h0im_ref,
               y_ref, hre_out, him_out, xre_ref, xim_ref, hre_s, him_s, *, n_seq, steps, carry):
    u = u_ref[...]
    ub = u.astype(BF16)
    cw = SSM_WIDTH // S5_GROUP_BLOCKS
    sw = SSM_FLAT // S5_GROUP_BLOCKS
    seg_len = steps // S5_SEGS if carry else steps
    pitch = seg_len + S5_SEG_PAD if carry else steps
    n_par = S5_SEGS if carry else n_seq

    def put_rows(ref, slab, val):
        if carry:
            for sg in range(S5_SEGS):
                ref[slab, sg * pitch:sg * pitch + seg_len, :] = val[sg * seg_len:(sg + 1) * seg_len]
        else:
            ref[slab] = val

    def get_rows(ref, slab):
        if carry:
            return jnp.concatenate([ref[slab, sg * pitch:sg * pitch + seg_len, :] for sg in range(S5_SEGS)], axis=0)
        return ref[slab]

    for j in range(S5_GROUP_BLOCKS):
        uj = ub[:, j * cw:(j + 1) * cw]
        bur = jnp.dot(uj, bre_ref[j], preferred_element_type=F32)
        bui = jnp.dot(uj, bim_ref[j], preferred_element_type=F32)
        zr = vec_ref[2:3, j * sw:(j + 1) * sw]
        zi = vec_ref[3:4, j * sw:(j + 1) * sw]
        xr = zr * bur - zi * bui
        xi = zr * bui + zi * bur
        for q in range(S5_BLOCK_LANES):
            put_rows(xre_ref, j * S5_BLOCK_LANES + q, xr[:, q * LANES:(q + 1) * LANES])
            put_rows(xim_ref, j * S5_BLOCK_LANES + q, xi[:, q * LANES:(q + 1) * LANES])

    if carry:
        @pl.when(pl.program_id(0) == 0)
        def _():
            hre_s[...] = h0re_ref[...]
            him_s[...] = h0im_ref[...]
    else:
        hre_s[...] = h0re_ref[...]
        him_s[...] = h0im_ref[...]

    scan_slabs = 2 * S5_BLOCK_LANES if carry else S5_BLOCK_LANES
    for j in range(S5_SLABS // scan_slabs):
        slabs = [j * scan_slabs + q for q in range(scan_slabs)]
        lanes = [slice(s * LANES, (s + 1) * LANES) for s in slabs]
        ar = [jnp.broadcast_to(vec_ref[0:1, l], (n_par, LANES)) for l in lanes]
        ai = [jnp.broadcast_to(vec_ref[1:2, l], (n_par, LANES)) for l in lanes]

        def step(t, h, slabs=slabs, ar=ar, ai=ai):
            rows = pl.ds(t, n_par, stride=pitch)
            out = []
            for q, s in enumerate(slabs):
                hr, hi = h[2 * q], h[2 * q + 1]
                nr = ar[q] * hr - ai[q] * hi + xre_ref[s, rows, :]
                ni = ar[q] * hi + ai[q] * hr + xim_ref[s, rows, :]
                xre_ref[s, rows, :] = nr
                xim_ref[s, rows, :] = ni
                out += [nr, ni]
            return tuple(out)

        h0 = []
        for l in lanes:
            h0 += ([jnp.zeros((n_par, LANES), F32)] * 2 if carry else [hre_s[:, l], him_s[:, l]])
        hT = lax.fori_loop(0, seg_len, step, tuple(h0), unroll=8)

        if not carry:
            for q, l in enumerate(lanes):
                hre_s[:, l] = hT[2 * q]
                him_s[:, l] = hT[2 * q + 1]
            continue

        cre, cim = [], []
        for q, l in enumerate(lanes):
            a_r, a_i = pw_ref[0, slabs[q], seg_len - 1:seg_len, :], pw_ref[1, slabs[q], seg_len - 1:seg_len, :]
            c_r, c_i = hre_s[:, l], him_s[:, l]
            rows_r, rows_i = [], []
            for sg in range(S5_SEGS):
                rows_r.append(c_r)
                rows_i.append(c_i)
                e_r, e_i = hT[2 * q][sg:sg + 1], hT[2 * q + 1][sg:sg + 1]
                c_r, c_i = e_r + a_r * c_r - a_i * c_i, e_i + a_r * c_i + a_i * c_r
            hre_s[:, l] = c_r
            him_s[:, l] = c_i
            cre.append(jnp.concatenate(rows_r, axis=0))
            cim.append(jnp.concatenate(rows_i, axis=0))

        def fix(t, carry_, slabs=slabs, lanes=lanes, cre=cre, cim=cim):
            rows = pl.ds(t, n_par, stride=pitch)
            for q, s in enumerate(slabs):
                p_r, p_i = pw_ref[0, s, pl.ds(t, 1), :], pw_ref[1, s, pl.ds(t, 1), :]
                xre_ref[s, rows, :] = xre_ref[s, rows, :] + (p_r * cre[q] - p_i * cim[q])
                xim_ref[s, rows, :] = xim_ref[s, rows, :] + (p_r * cim[q] + p_i * cre[q])
            return carry_

        lax.fori_loop(0, seg_len, fix, 0, unroll=8)

    hre_out[...] = hre_s[...]
    him_out[...] = him_s[...]

    ys = []
    for j in range(S5_GROUP_BLOCKS):
        slabs = range(j * S5_BLOCK_LANES, (j + 1) * S5_BLOCK_LANES)
        hr = jnp.concatenate([get_rows(xre_ref, s) for s in slabs], axis=-1).astype(BF16)
        hi = jnp.concatenate([get_rows(xim_ref, s) for s in slabs], axis=-1).astype(BF16)
        ys.append(jnp.dot(hr, cre_ref[j], preferred_element_type=F32)
                  - jnp.dot(hi, cim_ref[j], preferred_element_type=F32))
    y = jnp.concatenate(ys, axis=-1) + d_ref[...] * u
    z = _gelu(y)
    gate = _sigmoid(jnp.dot(z.astype(BF16), wglu_ref[...], preferred_element_type=F32))
    y_ref[...] = (z * gate).astype(BF16)


def _s5_mixer(proj, row0, n_rows, n_seq, steps, carry, wts, h0_re, h0_im):
    bre, bim, cre, cim, vec, pw, d, wglu = wts
    chunk = n_seq * steps
    n_chunks = n_rows // chunk
    blk0 = row0 // chunk
    st_rows = h0_re.shape[0]
    st_map = (lambda i: (0, 0)) if carry else (lambda i: (i, 0))
    fixed2 = lambda i: (0, 0)
    fixed3 = lambda i: (0, 0, 0)
    scratch_rows = S5_SEGS * (steps // S5_SEGS + S5_SEG_PAD) if carry else chunk
    kern = functools.partial(_s5_kernel, n_seq=n_seq, steps=steps, carry=carry)
    return pl.pallas_call(
        kern,
        grid=(n_chunks,),
        in_specs=[pl.BlockSpec((chunk, SSM_WIDTH), lambda i: (blk0 + i, 0)),
                  pl.BlockSpec(bre.shape, fixed3), pl.BlockSpec(bim.shape, fixed3),
                  pl.BlockSpec(cre.shape, fixed3), pl.BlockSpec(cim.shape, fixed3),
                  pl.BlockSpec(vec.shape, fixed2), pl.BlockSpec(pw.shape, lambda i: (0, 0, 0, 0)),
                  pl.BlockSpec(d.shape, fixed2),
                  pl.BlockSpec(wglu.shape, fixed2),
                  pl.BlockSpec((n_seq, SSM_FLAT), st_map), pl.BlockSpec((n_seq, SSM_FLAT), st_map)],
        out_specs=[pl.BlockSpec((chunk, SSM_WIDTH), lambda i: (i, 0)),
                   pl.BlockSpec((n_seq, SSM_FLAT), st_map), pl.BlockSpec((n_seq, SSM_FLAT), st_map)],
        out_shape=[jax.ShapeDtypeStruct((n_rows, SSM_WIDTH), BF16),
                   jax.ShapeDtypeStruct((st_rows, SSM_FLAT), F32),
                   jax.ShapeDtypeStruct((st_rows, SSM_FLAT), F32)],
        scratch_shapes=[pltpu.VMEM((S5_SLABS, scratch_rows, LANES), F32),
                        pltpu.VMEM((S5_SLABS, scratch_rows, LANES), F32),
                        pltpu.VMEM((n_seq, SSM_FLAT), F32), pltpu.VMEM((n_seq, SSM_FLAT), F32)],
        compiler_params=_cparams("arbitrary"),
        name="s5_mixer_carry" if carry else "s5_mixer_batch",
    )(proj, bre, bim, cre, cim, vec, pw, d, wglu, h0_re, h0_im)


def _s5_weights(a_re, a_im, log_dt, b_re, b_im, c_re, c_im, d, w_glu, seg_len):
    dt = jnp.exp(log_dt)[:, None]
    mag = jnp.exp(a_re * dt)
    abar_re, abar_im = mag * jnp.cos(a_im * dt), mag * jnp.sin(a_im * dt)
    den = a_re * a_re + a_im * a_im
    zr = ((abar_re - 1.0) * a_re + abar_im * a_im) / den
    zi = (abar_im * a_re - (abar_re - 1.0) * a_im) / den
    flat = lambda v: v.reshape(1, SSM_FLAT)
    vec = jnp.concatenate([flat(abar_re), flat(abar_im), flat(zr), flat(zi),
                           jnp.zeros((SUBLANES - 4, SSM_FLAT), F32)], axis=0)
    p_re, p_im, pows_re, pows_im = flat(abar_re), flat(abar_im), [], []
    for _ in range(seg_len):
        pows_re.append(p_re)
        pows_im.append(p_im)
        p_re, p_im = p_re * flat(abar_re) - p_im * flat(abar_im), p_re * flat(abar_im) + p_im * flat(abar_re)
    tail = [jnp.zeros((SUBLANES, SSM_FLAT), F32)]
    slabbed = lambda rows: jnp.concatenate(rows + tail, axis=0).reshape(-1, S5_SLABS, LANES).transpose(1, 0, 2)
    pw = jnp.stack([slabbed(pows_re), slabbed(pows_im)])
    gb = SSM_GROUPS // S5_GROUP_BLOCKS
    eye = jnp.eye(gb, dtype=F32)

    def pack_b(b):
        bb = b.reshape(S5_GROUP_BLOCKS, gb, SSM_STATE, SSM_GROUP_CH)
        m = jnp.einsum('jgph,gk->jghkp', bb, eye)
        return m.reshape(S5_GROUP_BLOCKS, gb * SSM_GROUP_CH, gb * SSM_STATE).astype(BF16)

    def pack_c(c):
        cc = c.reshape(S5_GROUP_BLOCKS, gb, SSM_GROUP_CH, SSM_STATE)
        m = jnp.einsum('jghp,gk->jgpkh', cc, eye)
        return m.reshape(S5_GROUP_BLOCKS, gb * SSM_STATE, gb * SSM_GROUP_CH).astype(BF16)

    return (pack_b(b_re), pack_b(b_im), pack_c(c_re), pack_c(c_im), vec, pw, d.reshape(1, SSM_WIDTH),
            w_glu.astype(BF16))


POOL_HIST = 16


def _pool_seq_kernel(u_ref, buf_ref, w_ref, scale_ref, y_ref, new_ref, xc_ref, *, chunk, pos0):
    i = pl.program_id(0)

    @pl.when(i == 0)
    def _():
        xc_ref[0:POOL_HIST, :] = buf_ref[...]

    u = u_ref[...]
    xc_ref[POOL_HIST:POOL_HIST + chunk, :] = u
    pos = pos0 + i * chunk + lax.broadcasted_iota(jnp.int32, (chunk, POOL_CH), 0)
    outs = []
    for g, wd in enumerate(POOL_WINDOWS):
        lanes = slice(g * POOL_CH, (g + 1) * POOL_CH)
        ug = u[:, lanes]
        acc = ug
        for k in range(1, wd):
            acc = acc + xc_ref[pl.ds(POOL_HIST - k, chunk), lanes]
        mix = acc / jnp.minimum(wd, pos + 1).astype(F32) - ug
        outs.append(jnp.dot(mix.astype(BF16), w_ref[g], preferred_element_type=F32))
    y_ref[...] = (jnp.concatenate(outs, axis=-1) * scale_ref[...]).astype(BF16)
    tail = xc_ref[chunk:chunk + POOL_HIST, :]
    xc_ref[0:POOL_HIST, :] = tail
    new_ref[...] = tail


def _pool_mixer_seq(proj, row0, n_rows, pos0, buf16, w, scale, chunk=256):
    blk0 = row0 // chunk
    kern = functools.partial(_pool_seq_kernel, chunk=chunk, pos0=pos0)
    return pl.pallas_call(
        kern,
        grid=(n_rows // chunk,),
        in_specs=[pl.BlockSpec((chunk, POOL_WIDTH), lambda i: (blk0 + i, 1)),
                  pl.BlockSpec((POOL_HIST, POOL_WIDTH), lambda i: (0, 0)),
                  pl.BlockSpec(w.shape, lambda i: (0, 0, 0)),
                  pl.BlockSpec((1, POOL_WIDTH), lambda i: (0, 0))],
        out_specs=[pl.BlockSpec((chunk, POOL_WIDTH), lambda i: (i, 0)),
                   pl.BlockSpec((POOL_HIST, POOL_WIDTH), lambda i: (0, 0))],
        out_shape=[jax.ShapeDtypeStruct((n_rows, POOL_WIDTH), BF16),
                   jax.ShapeDtypeStruct((POOL_HIST, POOL_WIDTH), F32)],
        scratch_shapes=[pltpu.VMEM((POOL_HIST + chunk, POOL_WIDTH), F32)],
        compiler_params=_cparams("arbitrary"),
        name="pool_mixer_seq",
    )(proj, buf16, w, scale)


def _pool_batch_kernel(u_ref, buf_ref, w_ref, scale_ref, y_ref, new_ref, *, steps, pos0):
    def xrow(j, lanes):
        return buf_ref[j, :, lanes] if j < POOL_BUF else u_ref[j - POOL_BUF, :, lanes]

    nb = u_ref.shape[1]
    for g, wd in enumerate(POOL_WINDOWS):
        lanes = slice(g * POOL_CH, (g + 1) * POOL_CH)
        mixes = []
        for t in range(steps):
            ug = u_ref[t, :, lanes]
            acc = ug
            for k in range(1, wd):
                acc = acc + xrow(POOL_BUF + t - k, lanes)
            mixes.append(acc / float(min(wd, pos0 + t + 1)) - ug)
        mix = jnp.concatenate(mixes, axis=0).astype(BF16)
        yg = jnp.dot(mix, w_ref[g], preferred_element_type=F32) * scale_ref[:, lanes]
        for t in range(steps):
            y_ref[t, :, lanes] = yg[t * nb:(t + 1) * nb].astype(BF16)
    for j in range(POOL_BUF):
        new_ref[j] = xrow(steps + j, slice(None))


def _pool_mixer_batch(u_t, buf_t, pos0, w, scale):
    steps, nb, _ = u_t.shape
    kern = functools.partial(_pool_batch_kernel, steps=steps, pos0=pos0)
    full3 = lambda i: (0, 0, 0)
    return pl.pallas_call(
        kern,
        grid=(1,),
        in_specs=[pl.BlockSpec(u_t.shape, full3), pl.BlockSpec(buf_t.shape, full3),
                  pl.BlockSpec(w.shape, full3), pl.BlockSpec((1, POOL_WIDTH), lambda i: (0, 0))],
        out_specs=[pl.BlockSpec(u_t.shape, full3), pl.BlockSpec(buf_t.shape, full3)],
        out_shape=[jax.ShapeDtypeStruct(u_t.shape, BF16), jax.ShapeDtypeStruct(buf_t.shape, F32)],
        compiler_params=_cparams("arbitrary"),
        name="pool_mixer_batch",
    )(u_t, buf_t, w, scale)


CMP_HALVES = KV_WIDTH // LANES
CMP_PAIR = LANES // HEAD_DIM
CMP_K = CMP_STRIDE * LANES
CMP_N = 2 * CMP_PAIR * CMP_HIDDEN


def _compress_kernel(*refs, n_seq, n_blk, transposed):
    if transposed:
        refs = refs[1:]
    blk_refs = refs[:n_seq * n_blk]
    wpair_ref, pe_ref, w1_ref, w2_ref, out_ref, xs_ref, g_ref, b_ref = refs[n_seq * n_blk:]
    blk_rows = blk_refs[0].shape[1] if transposed else blk_refs[0].shape[0]
    n = n_blk * blk_rows // CMP_STRIDE
    total = n_seq * n
    for i, r in enumerate(blk_refs):
        x = r[...].T if transposed else r[...]
        for h in range(CMP_HALVES):
            xs_ref[h, i * blk_rows:(i + 1) * blk_rows, :] = x[:, h * LANES:(h + 1) * LANES]
    ab = []
    for h in range(CMP_HALVES):
        for s_ in range(CMP_STRIDE):
            g_ref[h, :, s_ * LANES:(s_ + 1) * LANES] = (
                xs_ref[h, pl.ds(s_, total, stride=CMP_STRIDE), :].astype(BF16))
        ab.append(jnp.dot(g_ref[h], wpair_ref[...], preferred_element_type=F32))
    hid0 = jnp.dot(pe_ref[...], w1_ref[...], preferred_element_type=F32)[0:1]
    hid0 = jnp.concatenate([hid0] * CMP_PAIR, axis=-1)
    half_n = CMP_N // 2
    row = lax.broadcasted_iota(jnp.int32, (n, LANES), 0)
    b_ref[n:n + SUBLANES, :] = jnp.zeros((SUBLANES, half_n), F32)
    for h in range(CMP_HALVES):
        for q in range(n_seq):
            b_ref[0:n, :] = ab[h][q * n:(q + 1) * n, half_n:]
            hid = ab[h][q * n:(q + 1) * n, :half_n] + b_ref[pl.ds(1, n), :] + hid0
            out = jnp.dot(_gelu(hid).astype(BF16), w2_ref[...], preferred_element_type=F32)
            out_ref[q, :, h * LANES:(h + 1) * LANES] = jnp.where(row < n - 1, out, 0.0).astype(out_ref.dtype)


def _compress_weights(pe, w1, w2):
    w1b = w1.reshape(2, CMP_STRIDE, HEAD_DIM, CMP_HIDDEN)
    eye = jnp.eye(CMP_PAIR, dtype=F32)
    wpair = jnp.einsum('asdh,kj->skdajh', w1b, eye).reshape(CMP_K, CMP_N).astype(BF16)
    w2pair = jnp.einsum('hd,kj->khjd', w2, eye).reshape(CMP_PAIR * CMP_HIDDEN, LANES).astype(BF16)
    pe8 = jnp.concatenate([pe.reshape(1, CMP_BLOCK * HEAD_DIM),
                           jnp.zeros((SUBLANES - 1, CMP_BLOCK * HEAD_DIM), F32)], axis=0).astype(BF16)
    return wpair, pe8, w1.astype(BF16), w2pair


def _compress_scratch(n_seq, rows, n):
    return [pltpu.VMEM((CMP_HALVES, n_seq * rows, LANES), F32),
            pltpu.VMEM((CMP_HALVES, n_seq * n, CMP_K), BF16),
            pltpu.VMEM((n + SUBLANES, CMP_N // 2), F32)]


def _compress_seq(proj, col_block, n_rows, wts):
    n = n_rows // CMP_STRIDE
    full = lambda i: (0, 0)
    return pl.pallas_call(
        functools.partial(_compress_kernel, n_seq=1, n_blk=1, transposed=False),
        grid=(1,),
        in_specs=[pl.BlockSpec((n_rows, KV_WIDTH), lambda i: (0, col_block))]
        + [pl.BlockSpec(w.shape, full) for w in wts],
        out_specs=pl.BlockSpec((1, n, KV_WIDTH), lambda i: (0, 0, 0)),
        out_shape=jax.ShapeDtypeStruct((1, n, KV_WIDTH), BF16),
        scratch_shapes=_compress_scratch(1, n_rows, n),
        compiler_params=_cparams("arbitrary"),
        name="compress_seq",
    )(proj, *wts)[0]


def _compress_paged(cache_t, layer, page_table, wts, group=2):
    nb, n_pages = page_table.shape
    n = n_pages * PAGE_SIZE // CMP_STRIDE
    full = lambda b, pt: (0, 0)

    def page_spec(bl, p):
        return pl.BlockSpec((None, None, KV_WIDTH, PAGE_SIZE), lambda b, pt: (layer, pt[group * b + bl, p], 0, 0))

    grid_spec = pltpu.PrefetchScalarGridSpec(
        num_scalar_prefetch=1,
        grid=(nb // group,),
        in_specs=[page_spec(bl, p) for bl in range(group) for p in range(n_pages)]
        + [pl.BlockSpec(w.shape, full) for w in wts],
        out_specs=pl.BlockSpec((group, n, KV_WIDTH), lambda b, pt: (b, 0, 0)),
        scratch_shapes=_compress_scratch(group, n_pages * PAGE_SIZE, n),
    )
    return pl.pallas_call(
        functools.partial(_compress_kernel, n_seq=group, n_blk=n_pages, transposed=True),
        grid_spec=grid_spec,
        out_shape=jax.ShapeDtypeStruct((nb, n, KV_WIDTH), BF16),
        compiler_params=_cparams("arbitrary"),
        name="compress_paged",
    )(page_table, *([cache_t] * (group * n_pages)), *wts)


def _alibi_slope(h):
    return 2.0 ** (-8.0 * (h + 1) / N_HEADS)


NT_DIMS = (((1,), (1,)), ((), ()))
QROWS = GQA * Q_BLOCK
KEY_TILE = 128
WIN_TILES = WINDOW // KEY_TILE
KEY_STEP = 2 * KEY_TILE
T_FULL, T_DIAG, T_OLD, T_NONE = 0, 1, 2, 3
LOG2E = math.log2(math.e)
V_ROWS = HEAD_DIM + 16
CMP_TILES_PER_SLAB = LANES * CMP_STRIDE // Q_BLOCK


def _nsa_seq_kernel(q_ref, gate_ref, kc_ref, vct_ref, ks_ref, vst_ref, kw_ref, vwt_ref, o_ref,
                    qx_ref, bias_ref, sbuf_ref, m_ref, acc_ref, oc_ref, os_ref, ow_ref, *, n_cmp_pad, n_sel, n_steps):
    i = pl.program_id(0)
    base = i * Q_BLOCK

    lane = lax.broadcasted_iota(jnp.int32, (1, QROWS), 1)
    qq_i = lane % Q_BLOCK
    qq = qq_i.astype(F32)
    g_lane = lane // Q_BLOCK

    def slope_row(k):
        r = jnp.full((1, QROWS), LOG2E * _alibi_slope(GQA * k + GQA - 1), F32)
        for g in range(GQA - 1):
            r = jnp.where(g_lane == g, LOG2E * _alibi_slope(GQA * k + g), r)
        return r

    slopes = [slope_row(k) for k in range(N_KV)]

    @pl.when(i == 0)
    def _():
        kk = lax.broadcasted_iota(jnp.int32, (KEY_TILE, QROWS), 0).astype(F32)
        for half in range(2):
            for k in range(N_KV):
                b = -slopes[k] * (qq - (kk + float(half * KEY_TILE)))
                bias_ref[half, T_FULL, k] = b
                bias_ref[half, T_DIAG, k] = jnp.where(kk <= qq, b, NEG_INF)
                bias_ref[half, T_OLD, k] = jnp.where(kk > qq, b, NEG_INF)
                bias_ref[half, T_NONE, k] = jnp.full((KEY_TILE, QROWS), NEG_INF, F32)

    qs = q_ref[...] * (LOG2E * HEAD_DIM ** -0.5)
    low_half = lax.broadcasted_iota(jnp.int32, (Q_BLOCK, LANES), 1) < HEAD_DIM
    for k in range(N_KV):
        keep = low_half if k % 2 == 0 else jnp.logical_not(low_half)
        parts = []
        for g in range(GQA):
            h = GQA * k + g
            pair = qs[:, (h // 2) * LANES:(h // 2 + 1) * LANES]
            if h % 2 != k % 2:
                pair = pltpu.roll(pair, HEAD_DIM, 1)
            parts.append(jnp.where(keep, pair, 0.0).astype(BF16))
        qx_ref[k, :, 0:LANES] = jnp.concatenate(parts, axis=0)

    sjf = lax.broadcasted_iota(jnp.int32, (n_sel, Q_BLOCK), 0).astype(F32)
    curf = ((base + lax.broadcasted_iota(jnp.int32, (1, Q_BLOCK), 1)) // SEL_BLOCK).astype(F32)
    visible = sjf <= curf
    forced = visible & ((sjf == 0.0) | (sjf == curf) | (sjf == curf - 1.0))

    def compressed_and_select(n_vis):
        n_io = lax.broadcasted_iota(jnp.int32, (n_vis, QROWS), 0)
        distc = (base.astype(F32) + qq) - (n_io.astype(F32) * CMP_STRIDE + (CMP_BLOCK - 1) / 2.0)
        visc = (n_io * CMP_STRIDE + (CMP_BLOCK - 1)) <= (base + qq_i)
        sj_o = lax.broadcasted_iota(jnp.int32, (n_sel, n_vis), 0)
        nn_o = lax.broadcasted_iota(jnp.int32, (n_sel, n_vis), 1)
        ovt = jnp.where((nn_o * CMP_STRIDE < (sj_o + 1) * SEL_BLOCK)
                        & (nn_o * CMP_STRIDE + (CMP_BLOCK - 1) >= sj_o * SEL_BLOCK), 1.0, 0.0).astype(BF16)
        for k in range(N_KV):
            kl = slice((k // 2) * LANES, (k // 2 + 1) * LANES)
            vr = slice(k * HEAD_DIM, (k + 1) * HEAD_DIM)
            s = lax.dot_general(kc_ref[0:n_vis, kl], qx_ref[k, :, 0:LANES], NT_DIMS, preferred_element_type=F32)
            s = jnp.where(visc, s - slopes[k] * distc, NEG_INF)
            m = jnp.maximum(jnp.max(s, axis=0, keepdims=True), 0.5 * NEG_INF)
            p = jnp.exp2(s - m)
            p = p * (1.0 / jnp.maximum(jnp.sum(p, axis=0, keepdims=True), TINY))
            oc_ref[k] = jnp.dot(vct_ref[vr, 0:n_vis], p.astype(BF16), preferred_element_type=F32)
            psum = p[:, 0:Q_BLOCK]
            for g in range(1, GQA):
                psum = psum + p[:, g * Q_BLOCK:(g + 1) * Q_BLOCK]
            p_hi = psum.astype(BF16)
            p_lo = (psum - p_hi.astype(F32)).astype(BF16)
            imp = (jnp.dot(ovt, p_hi, preferred_element_type=F32)
                   + jnp.dot(ovt, p_lo, preferred_element_type=F32))
            score = jnp.where(forced, FORCED_SCORE, jnp.where(visible, imp, -1.0))
            sel = jnp.zeros_like(score)
            for _ in range(min(TOP_N, n_sel)):
                mx = jnp.max(score, axis=0, keepdims=True)
                idx = jnp.min(jnp.where(score == mx, sjf, 1e9), axis=0, keepdims=True)
                hit = sjf == idx
                sel = jnp.where(hit & (mx >= 0.0), 1.0, sel)
                score = jnp.where(hit, -2.0, score)
            selq = jnp.where(sel > 0.0, 0.0, NEG_INF).T.astype(BF16)
            if n_sel < LANES:
                selq = jnp.concatenate([selq, jnp.zeros((Q_BLOCK, LANES - n_sel), BF16)], axis=1)
            qx_ref[k, :, LANES:2 * LANES] = jnp.concatenate([selq] * GQA, axis=0)

    n_slabs = -(-n_cmp_pad // LANES)
    for v in range(n_slabs):
        pl.when(i // CMP_TILES_PER_SLAB == v)(
            functools.partial(compressed_and_select, min((v + 1) * LANES, n_cmp_pad)))

    blk_of_key = lax.broadcasted_iota(jnp.int32, (KEY_STEP, LANES), 0) // SEL_BLOCK
    lane_id = lax.broadcasted_iota(jnp.int32, (KEY_STEP, LANES), 1)

    def flash(k_ref, vt_ref, j_lo, use_sel, out_ref):
        m_ref[...] = jnp.full(m_ref.shape, NEG_INF, F32)
        acc_ref[...] = jnp.zeros(acc_ref.shape, F32)

        def tile_type(t):
            ty = jnp.where(t == i, T_DIAG, jnp.where(t > i, T_NONE, T_FULL))
            if not use_sel:
                ty = jnp.where(t == i - WIN_TILES, T_OLD, jnp.where(t < i - WIN_TILES, T_NONE, ty))
            return ty

        def scores(j, heads):
            jj = jnp.minimum(j, n_steps - 1)
            ty0, ty1 = tile_type(2 * jj), tile_type(2 * jj + 1)
            if use_sel:
                onehot = jnp.where(lane_id == blk_of_key + (KEY_STEP // SEL_BLOCK) * jj, 1.0, 0.0).astype(BF16)
            out = []
            for k in heads:
                kl = slice((k // 2) * LANES, (k // 2 + 1) * LANES)
                if use_sel:
                    kx = jnp.concatenate([k_ref[jj, :, kl], onehot], axis=1)
                    s = lax.dot_general(kx, qx_ref[k], NT_DIMS, preferred_element_type=F32)
                else:
                    s = lax.dot_general(k_ref[jj, :, kl], qx_ref[k, :, 0:LANES], NT_DIMS,
                                        preferred_element_type=F32)
                out.append(s + jnp.concatenate([bias_ref[0, ty0, k], bias_ref[1, ty1, k]], axis=0))
            return out

        def softmax_pv(j, k, s):
            c = slopes[k] * (i * KEY_TILE - j * KEY_STEP).astype(F32)
            m_old = m_ref[k] + c
            m_new = jnp.maximum(m_old, jnp.max(s, axis=0, keepdims=True))
            alpha = jnp.exp2(m_old - m_new)
            p = jnp.exp2(s - m_new).astype(BF16)
            m_ref[k] = m_new - c
            acc_ref[k] = alpha * acc_ref[k] + jnp.dot(vt_ref[j, k * V_ROWS:(k + 1) * V_ROWS, :], p,
                                                      preferred_element_type=F32)

        sbuf_ref[0], sbuf_ref[1] = scores(j_lo, (0, 1))

        def body(j, carry):
            (s2,) = scores(j, (2,))
            softmax_pv(j, 0, sbuf_ref[0])
            (s3,) = scores(j, (3,))
            softmax_pv(j, 1, sbuf_ref[1])
            n0, n1 = scores(j + 1, (0, 1))
            sbuf_ref[0] = n0
            softmax_pv(j, 2, s2)
            sbuf_ref[1] = n1
            softmax_pv(j, 3, s3)
            return carry

        lax.fori_loop(j_lo, i // 2 + 1, body, 0)
        for k in range(N_KV):
            out_ref[k] = acc_ref[k, 0:HEAD_DIM] * (1.0 / jnp.maximum(acc_ref[k, HEAD_DIM:HEAD_DIM + 1], TINY))

    flash(ks_ref, vst_ref, 0, True, os_ref)
    flash(kw_ref, vwt_ref, jnp.maximum(i - WIN_TILES, 0) // 2, False, ow_ref)

    gate = _sigmoid(gate_ref[...])
    blocks = []
    for k in range(N_KV):
        for g in range(GQA):
            h = GQA * k + g
            sl = slice(g * Q_BLOCK, (g + 1) * Q_BLOCK)
            blocks.append(oc_ref[k, :, sl] * gate[3 * h:3 * h + 1]
                          + os_ref[k, :, sl] * gate[3 * h + 1:3 * h + 2]
                          + ow_ref[k, :, sl] * gate[3 * h + 2:3 * h + 3])
    o_ref[...] = jnp.concatenate(blocks, axis=0).T.astype(BF16)


def _value_steps(v):
    steps = v.shape[0] // KEY_STEP
    vt = v.reshape(steps, KEY_STEP, N_KV, HEAD_DIM).transpose(0, 2, 3, 1)
    ones = jnp.ones((steps, N_KV, 1, KEY_STEP), BF16)
    pad = jnp.zeros((steps, N_KV, V_ROWS - HEAD_DIM - 1, KEY_STEP), BF16)
    return jnp.concatenate([vt, ones, pad], axis=2).reshape(steps, N_KV * V_ROWS, KEY_STEP)


def _nsa_seq(proj, gate_t, kc, vct, ks_t, vst_t, kw_t, vwt_t, n_rows):
    n_tiles = n_rows // Q_BLOCK
    assert n_tiles % 2 == 0
    n_cmp_pad = kc.shape[0]
    n_sel = n_rows // SEL_BLOCK
    assert n_sel <= LANES
    kern = functools.partial(_nsa_seq_kernel, n_cmp_pad=n_cmp_pad, n_sel=n_sel, n_steps=n_tiles // 2)
    c2 = lambda i: (0, 0)
    c3 = lambda i: (0, 0, 0)
    st = (N_KV, HEAD_DIM, QROWS)
    return pl.pallas_call(
        kern,
        grid=(n_tiles,),
        in_specs=[pl.BlockSpec((Q_BLOCK, NSA_WIDTH), lambda i: (i, 1)),
                  pl.BlockSpec((3 * N_HEADS, Q_BLOCK), lambda i: (0, i)),
                  pl.BlockSpec(kc.shape, c2), pl.BlockSpec(vct.shape, c2),
                  pl.BlockSpec(ks_t.shape, c3), pl.BlockSpec(vst_t.shape, c3),
                  pl.BlockSpec(kw_t.shape, c3), pl.BlockSpec(vwt_t.shape, c3)],
        out_specs=pl.BlockSpec((Q_BLOCK, NSA_WIDTH), lambda i: (i, 0)),
        out_shape=jax.ShapeDtypeStruct((n_rows, NSA_WIDTH), BF16),
        scratch_shapes=[pltpu.VMEM((N_KV, QROWS, 2 * LANES), BF16),
                        pltpu.VMEM((2, 4, N_KV, KEY_TILE, QROWS), F32),
                        pltpu.VMEM((2, KEY_STEP, QROWS), F32),
                        pltpu.VMEM((N_KV, 1, QROWS), F32), pltpu.VMEM((N_KV, V_ROWS, QROWS), F32),
                        pltpu.VMEM(st, F32), pltpu.VMEM(st, F32), pltpu.VMEM(st, F32)],
        compiler_params=_cparams("arbitrary"),
        name="nsa_seq",
    )(proj, gate_t, kc, vct, ks_t, vst_t, kw_t, vwt_t)


def _nsa_batch_kernel(pt_ref, *refs, n_pages, steps, pos0, group, n_prev):
    del pt_ref
    refs, prev_refs, out_refs = (refs[:2 * group * n_pages + 10], refs[2 * group * n_pages + 10:][:2 * n_prev],
                                 refs[2 * group * n_pages + 10 + 2 * n_prev:])
    kwin_ref, vwin_ref, kc_ref, vc_ref, q_ref, ksn_ref, vsn_ref, kwn_ref, vwn_ref, gate_ref = refs[2 * group * n_pages:]
    o_ref, kwo_ref, vwo_ref, bsel_ref, bwin_ref, bcmp_ref, e_ref, ov_ref = out_refs
    for j in range(n_prev):
        kwo_ref[j] = prev_refs[j][...]
        vwo_ref[j] = prev_refs[n_prev + j][...]
    rows = N_HEADS * steps
    past = n_pages * PAGE_SIZE
    nk = past + KEY_TILE
    nw = WINDOW + KEY_TILE

    @pl.when(pl.program_id(0) == 0)
    def _():
        def tables(width):
            r = lax.broadcasted_iota(jnp.int32, (rows, width), 0)
            c = lax.broadcasted_iota(jnp.int32, (rows, width), 1)
            slope = jnp.exp((-8.0 * math.log(2.0) / N_HEADS) * (r // steps + 1).astype(F32))
            return slope, pos0 + r % steps, c

        slope, qpos, key = tables(nk)
        bsel_ref[...] = jnp.where(key <= qpos, -slope * (qpos - key).astype(F32), NEG_INF)
        slope, qpos, w = tables(nw)
        kpos = pos0 - WINDOW + w
        d = qpos - kpos
        bwin_ref[...] = jnp.where((d >= 0) & (d < WINDOW) & (kpos >= 0), -slope * d.astype(F32), NEG_INF)
        slope, qpos, n = tables(LANES)
        c_mid = n.astype(F32) * CMP_STRIDE + (CMP_BLOCK - 1) / 2.0
        bcmp_ref[...] = jnp.where(n * CMP_STRIDE + (CMP_BLOCK - 1) <= qpos,
                                  -slope * (qpos.astype(F32) - c_mid), NEG_INF)
        sj = lax.broadcasted_iota(jnp.int32, (LANES, nk), 0)
        key = lax.broadcasted_iota(jnp.int32, (LANES, nk), 1)
        e_ref[...] = jnp.where(key // SEL_BLOCK == sj, 1.0, 0.0).astype(BF16)
        n = lax.broadcasted_iota(jnp.int32, (LANES, LANES), 0)
        sj = lax.broadcasted_iota(jnp.int32, (LANES, LANES), 1)
        ov_ref[...] = jnp.where((n * CMP_STRIDE < (sj + 1) * SEL_BLOCK)
                                & (n * CMP_STRIDE + (CMP_BLOCK - 1) >= sj * SEL_BLOCK), 1.0, 0.0).astype(BF16)

    def _nsa_batch_one(bl):
        ks_pages = refs[bl * n_pages:(bl + 1) * n_pages]
        vs_pages = refs[(group + bl) * n_pages:(group + bl + 1) * n_pages]
        tok = slice(bl * steps, (bl + 1) * steps)
        qs = q_ref[tok, :] * (HEAD_DIM ** -0.5)
        low_half = lax.broadcasted_iota(jnp.int32, (steps, LANES), 1) < HEAD_DIM
        zero = jnp.zeros((steps, LANES), F32)
        pieces = []
        for k in range(N_KV):
            keep = low_half if k % 2 == 0 else jnp.logical_not(low_half)
            for g in range(GQA):
                h = GQA * k + g
                pair = qs[:, (h // 2) * LANES:(h // 2 + 1) * LANES]
                if h % 2 != k % 2:
                    pair = pltpu.roll(pair, HEAD_DIM, 1)
                blk = jnp.where(keep, pair, 0.0)
                pieces.append(jnp.concatenate([blk, zero] if k // 2 == 0 else [zero, blk], axis=1))
        qb = jnp.concatenate(pieces, axis=0).astype(BF16)

        def new_tile(ref):
            return jnp.concatenate([ref[tok, :], jnp.zeros((KEY_TILE - steps, KV_WIDTH), F32)], axis=0).astype(BF16)

        def softmax_pv(s, v_parts, transposed):
            m = jnp.max(s, axis=-1, keepdims=True)
            p = jnp.exp(s - m)
            l = jnp.sum(p, axis=-1, keepdims=True)
            acc = None
            off = 0
            for v, v_t in zip(v_parts, transposed):
                n = v.shape[1] if v_t else v.shape[0]
                pb = p[:, off:off + n].astype(BF16)
                part = (lax.dot_general(pb, v, NT_DIMS, preferred_element_type=F32) if v_t
                        else jnp.dot(pb, v, preferred_element_type=F32))
                acc = part if acc is None else acc + part
                off += n
            return acc * (1.0 / jnp.maximum(l, TINY))

        bc = bcmp_ref[...]
        s_c = lax.dot_general(qb, kc_ref[bl], NT_DIMS, preferred_element_type=F32) + bc
        kt_all = jnp.concatenate([r[...] for r in ks_pages], axis=1).astype(BF16)
        s_s = jnp.concatenate([jnp.dot(qb, kt_all, preferred_element_type=F32),
                               lax.dot_general(qb, new_tile(ksn_ref), NT_DIMS, preferred_element_type=F32)],
                              axis=1) + bsel_ref[...]
        s_w = jnp.concatenate([jnp.dot(qb, kwin_ref[bl].astype(BF16), preferred_element_type=F32),
                               lax.dot_general(qb, new_tile(kwn_ref), NT_DIMS, preferred_element_type=F32)],
                              axis=1) + bwin_ref[...]
        yield

        m = jnp.max(s_c, axis=-1, keepdims=True)
        p = jnp.exp(s_c - m) * jnp.where(bc > 0.5 * NEG_INF, 1.0, 0.0)
        p = p * (1.0 / jnp.maximum(jnp.sum(p, axis=-1, keepdims=True), TINY))
        o_c = jnp.dot(p.astype(BF16), vc_ref[bl], preferred_element_type=F32)
        grp = GQA * steps
        psum = []
        for k in range(N_KV):
            acc = p[k * grp:k * grp + steps]
            for g in range(1, GQA):
                acc = acc + p[k * grp + g * steps:k * grp + (g + 1) * steps]
            psum.append(acc)
        psum = jnp.concatenate(psum, axis=0)
        p_hi = psum.astype(BF16)
        p_lo = (psum - p_hi.astype(F32)).astype(BF16)
        imp = (jnp.dot(p_hi, ov_ref[...], preferred_element_type=F32)
               + jnp.dot(p_lo, ov_ref[...], preferred_element_type=F32))
        yield

        o_w = softmax_pv(s_w, [vwin_ref[bl].astype(BF16), new_tile(vwn_ref)], [True, False])
        lane_w = lax.broadcasted_iota(jnp.int32, (KV_WIDTH, WINDOW), 1)
        for src, nw_ref, dst in ((kwin_ref, kwn_ref, kwo_ref), (vwin_ref, vwn_ref, vwo_ref)):
            new_t = jnp.concatenate([nw_ref[tok, :], jnp.zeros((LANES - steps, KV_WIDTH), F32)], axis=0).T
            new_t = jnp.concatenate([new_t] * (WINDOW // LANES), axis=1)
            shifted = pltpu.roll(src[bl], WINDOW - steps, 1)
            dst[n_prev, bl] = jnp.where(lane_w < WINDOW - steps, shifted, pltpu.roll(new_t, WINDOW - steps, 1))
        yield

        sjf = lax.broadcasted_iota(jnp.int32, imp.shape, 1).astype(F32)
        step_of_row = lax.broadcasted_iota(jnp.int32, imp.shape, 0) % steps
        curf = ((pos0 + step_of_row) // SEL_BLOCK).astype(F32)
        visible = sjf <= curf
        forced = visible & ((sjf == 0.0) | (sjf == curf) | (sjf == curf - 1.0))
        score = jnp.where(forced, FORCED_SCORE, jnp.where(visible, imp, -1.0))
        n_blocks = -(-(past + steps) // SEL_BLOCK)
        rank = jnp.zeros_like(score)
        for jp in range(n_blocks):
            other = jnp.broadcast_to(score[:, jp:jp + 1], score.shape)
            beats = (other > score) | ((other == score) & (sjf > float(jp)))
            rank = rank + jnp.where(beats, 1.0, 0.0)
        sel = jnp.where((rank < float(TOP_N)) & (score >= 0.0), 1.0, 0.0)
        selk = jnp.dot(sel.astype(BF16), e_ref[...], preferred_element_type=F32)
        yield

        selb = jnp.where(selk > 0.5, 0.0, NEG_INF)
        sb = jnp.concatenate([selb[k * steps:(k + 1) * steps] for k in range(N_KV) for _ in range(GQA)], axis=0)
        vt_all = jnp.concatenate([r[...] for r in vs_pages], axis=1).astype(BF16)
        o_s = softmax_pv(s_s + sb, [vt_all, new_tile(vsn_ref)], [True, False])
        yield

        gates = _sigmoid(gate_ref[bl])
        y = o_c * gates[:, 0:1] + o_s * gates[:, 1:2] + o_w * gates[:, 2:3]
        for h in range(N_HEADS):
            k = h // GQA
            o_ref[tok, h * HEAD_DIM:(h + 1) * HEAD_DIM] = y[h * steps:(h + 1) * steps, k * HEAD_DIM:(k + 1) * HEAD_DIM]

    pending = [_nsa_batch_one(bl) for bl in range(group)]
    while pending:
        pending = [g for g in pending if next(g, True) is None]


def _nsa_batch(proj, row0, steps, pos0, gate_rows, kc, vc, ks_cache, vs_cache, layer, page_table, kwin, vwin,
               prev_kw, prev_vw, group=2):
    nb, n_pages = page_table.shape
    rows = N_HEADS * steps
    nk = n_pages * PAGE_SIZE + KEY_TILE
    tok = group * steps
    blk0 = row0 // tok
    n_prev = len(prev_kw)
    kern = functools.partial(_nsa_batch_kernel, n_pages=n_pages, steps=steps, pos0=pos0, group=group, n_prev=n_prev)

    def page_spec(bl, p):
        return pl.BlockSpec((None, None, KV_WIDTH, PAGE_SIZE), lambda b, pt: (layer, pt[group * b + bl, p], 0, 0))

    pages = [page_spec(bl, p) for bl in range(group) for p in range(n_pages)]

    per_b = lambda b, pt: (b, 0, 0)
    kv_col0 = sum(PROJ_SIZES[:3]) // KV_WIDTH

    def new_spec(j):
        return pl.BlockSpec((tok, KV_WIDTH), lambda b, pt: (blk0 + b, kv_col0 + j))

    grid_spec = pltpu.PrefetchScalarGridSpec(
        num_scalar_prefetch=1,
        grid=(nb // group,),
        in_specs=pages + pages
        + [pl.BlockSpec((None, group, KV_WIDTH, WINDOW), lambda b, pt: (layer, b, 0, 0)),
           pl.BlockSpec((None, group, KV_WIDTH, WINDOW), lambda b, pt: (layer, b, 0, 0)),
           pl.BlockSpec((group,) + kc.shape[1:], per_b), pl.BlockSpec((group,) + vc.shape[1:], per_b),
           pl.BlockSpec((tok, NSA_WIDTH), lambda b, pt: (blk0 + b, 1)),
           new_spec(2), new_spec(3), new_spec(4), new_spec(5),
           pl.BlockSpec((group, rows, SUBLANES), per_b)]
        + [pl.BlockSpec((group, KV_WIDTH, WINDOW), per_b)] * (2 * n_prev),
        out_specs=[pl.BlockSpec((tok, NSA_WIDTH), lambda b, pt: (b, 0)),
                   pl.BlockSpec((n_prev + 1, group, KV_WIDTH, WINDOW), lambda b, pt: (0, b, 0, 0)),
                   pl.BlockSpec((n_prev + 1, group, KV_WIDTH, WINDOW), lambda b, pt: (0, b, 0, 0))],
        scratch_shapes=[pltpu.VMEM((rows, nk), F32), pltpu.VMEM((rows, WINDOW + KEY_TILE), F32),
                        pltpu.VMEM((rows, LANES), F32), pltpu.VMEM((LANES, nk), BF16),
                        pltpu.VMEM((LANES, LANES), BF16)],
    )
    return pl.pallas_call(
        kern,
        grid_spec=grid_spec,
        out_shape=[jax.ShapeDtypeStruct((nb * steps, NSA_WIDTH), F32),
                   jax.ShapeDtypeStruct((n_prev + 1, nb, KV_WIDTH, WINDOW), F32),
                   jax.ShapeDtypeStruct((n_prev + 1, nb, KV_WIDTH, WINDOW), F32)],
        compiler_params=_cparams("arbitrary"),
        name="nsa_batch",
    )(page_table, *([ks_cache] * (group * n_pages)), *([vs_cache] * (group * n_pages)), kwin, vwin, kc, vc,
      proj, proj, proj, proj, proj, gate_rows, *prev_kw, *prev_vw)


S5_SEQ_CHUNK = 256
S5_BATCH_SEQS = 16


def kernel(x_prompt, x_sample, cache_k_cmp, cache_v_cmp, cache_k_sel, cache_v_sel, cache_k_win, cache_v_win, state_ssm_re, state_ssm_im, state_pool, page_table, w_in, ssm_a_re, ssm_a_im, ssm_log_dt, ssm_b_re, ssm_b_im, ssm_c_re, ssm_c_im, ssm_d, ssm_w_glu, pool_w, pool_scale, cmp_pe, cmp_w1, cmp_w2, w_out, ln1_g, ln1_b, mlp_w1, mlp_w2, ln2_g, ln2_b):
    bp, lp, d = x_prompt.shape
    nb, steps, _ = x_sample.shape
    assert bp == 1 and d == D_MODEL and steps < CMP_STRIDE
    n_p, n_s = bp * lp, nb * steps
    n_phys = cache_k_cmp.shape[1]
    past_len = page_table.shape[1] * PAGE_SIZE
    assert cache_k_win.shape[2] == WINDOW and past_len >= WINDOW

    x = jnp.concatenate([x_prompt.reshape(n_p, d), x_sample.reshape(n_s, d)], axis=0)
    rows_last = lambda c: c.transpose(0, 1, 3, 4, 2).reshape(c.shape[0], c.shape[1], KV_WIDTH, c.shape[2])
    kcmp_pages, vcmp_pages = rows_last(cache_k_cmp), rows_last(cache_v_cmp)
    ksel_pages, vsel_pages = rows_last(cache_k_sel), rows_last(cache_v_sel)
    kwin_t, vwin_t = rows_last(cache_k_win), rows_last(cache_v_win)
    kv_col0 = sum(PROJ_SIZES[:3])
    zero_state = jnp.zeros((1, SSM_FLAT), F32)
    zero_buf = jnp.zeros((POOL_HIST, POOL_WIDTH), F32)
    row = lambda v: v.reshape(1, -1)

    w_in_pad = jnp.pad(w_in, ((0, 0), (0, 0), (0, PROJ_PAD - PROJ_WIDTH)))

    new_p, new_s = [], []
    kw_s = vw_s = None
    for l in range(DEPTH):
        proj, proj_b = _input_projection(x, w_in_pad, l)
        kv = [proj[:, kv_col0 + j * KV_WIDTH:kv_col0 + (j + 1) * KV_WIDTH] for j in range(6)]
        kv_b = [proj_b[:n_p, kv_col0 + j * KV_WIDTH:kv_col0 + (j + 1) * KV_WIDTH] for j in range(6)]
        gate_logits = proj[:, GATE_COL:GATE_COL + 3 * N_HEADS]

        s5w = _s5_weights(ssm_a_re[l], ssm_a_im[l], ssm_log_dt[l], ssm_b_re[l], ssm_b_im[l], ssm_c_re[l],
                          ssm_c_im[l], ssm_d[l], ssm_w_glu[l], S5_SEQ_CHUNK // S5_SEGS)
        ys_p, hr_p, hi_p = _s5_mixer(proj, 0, n_p, 1, S5_SEQ_CHUNK, True, s5w, zero_state, zero_state)
        ys_s, hr_s, hi_s = _s5_mixer(proj, n_p, n_s, S5_BATCH_SEQS, steps, False, s5w,
                                     state_ssm_re[l].reshape(nb, SSM_FLAT), state_ssm_im[l].reshape(nb, SSM_FLAT))

        pw, psc = pool_w[l].astype(BF16), row(pool_scale[l])
        yp_p, pool_p = _pool_mixer_seq(proj, 0, n_p, 0, zero_buf, pw, psc)
        u_t = proj[n_p:, SSM_WIDTH:SSM_WIDTH + POOL_WIDTH].reshape(nb, steps, POOL_WIDTH).transpose(1, 0, 2)
        yp_s_t, pool_s_t = _pool_mixer_batch(u_t, state_pool[l].transpose(1, 0, 2), past_len, pw, psc)
        yp_s = yp_s_t.transpose(1, 0, 2).reshape(n_s, POOL_WIDTH)

        cw_k = _compress_weights(cmp_pe[l, 0], cmp_w1[l, 0], cmp_w2[l, 0])
        cw_v = _compress_weights(cmp_pe[l, 1], cmp_w1[l, 1], cmp_w2[l, 1])
        kc_p = _compress_seq(proj, kv_col0 // KV_WIDTH, n_p, cw_k)
        vc_p = _compress_seq(proj, kv_col0 // KV_WIDTH + 1, n_p, cw_v)
        tiles = lambda a: a.reshape(n_p // KEY_STEP, KEY_STEP, KV_WIDTH)
        yn_p = _nsa_seq(proj, gate_logits[:n_p].T, kc_p, vc_p.T, tiles(kv_b[2]), _value_steps(kv_b[3]),
                        tiles(kv_b[4]), _value_steps(kv_b[5]), n_p)

        kc_s = _compress_paged(kcmp_pages, l, page_table, cw_k)
        vc_s = _compress_paged(vcmp_pages, l, page_table, cw_v)
        gate_rows = gate_logits[n_p:].reshape(nb, steps, N_HEADS, 3).transpose(0, 2, 1, 3)
        gate_rows = jnp.pad(gate_rows.reshape(nb, N_HEADS * steps, 3), ((0, 0), (0, 0), (0, SUBLANES - 3)))
        yn_s, kw_s, vw_s = _nsa_batch(proj, n_p, steps, past_len, gate_rows, kc_s, vc_s, ksel_pages, vsel_pages, l,
                                      page_table, kwin_t, vwin_t, [kw_s[j] for j in range(l)], [vw_s[j] for j in range(l)])

        x = _output_projection_ln(x, n_p, (ys_p, ys_s), (yp_p, yp_s), (yn_p, yn_s), w_out, l,
                                  row(ln1_g[l]), row(ln1_b[l]))
        x = _mlp_ln(x, mlp_w1, mlp_w2, l, row(ln2_g[l]), row(ln2_b[l]))

        heads = lambda a, b_, t: a.reshape(b_, t, N_KV, HEAD_DIM)
        n_keep = min(WINDOW, lp)
        new_p.append([heads(kv[j][:n_p], bp, lp) for j in range(4)]
                     + [heads(kv[j][n_p - n_keep:n_p], bp, n_keep) for j in (4, 5)]
                     + [hr_p.reshape(bp, SSM_GROUPS, SSM_STATE), hi_p.reshape(bp, SSM_GROUPS, SSM_STATE),
                        pool_p[POOL_HIST - POOL_BUF:].reshape(bp, POOL_BUF, POOL_WIDTH)])
        new_s.append([heads(kv[j][n_p:], nb, steps) for j in range(4)]
                     + [hr_s.reshape(nb, SSM_GROUPS, SSM_STATE), hi_s.reshape(nb, SSM_GROUPS, SSM_STATE),
                        pool_s_t.transpose(1, 0, 2)])

    st_p = [jnp.stack(f) for f in zip(*new_p)]
    st_s = [jnp.stack(f) for f in zip(*new_s)]
    rows_first = lambda a: a.reshape(DEPTH, nb, N_KV, HEAD_DIM, WINDOW).transpose(0, 1, 4, 2, 3)
    st_s[4:4] = [rows_first(kw_s), rows_first(vw_s)]
    out = [x[:n_p].reshape(bp, lp, d), x[n_p:].reshape(nb, steps, d)]
    for a, b_ in zip(st_p, st_s):
        out += [a, b_]
    return tuple(out)
```
